```python
import jax, jax.numpy as jnp
from jax import lax
import numpy as np

D_MODEL = 4096
BATCH = 2
SEQ = 8192
DEPTH = 1

CHUNK = 64
D_MIX = D_MODEL
GDN_HEAD_DIM = 128
GDN_HEADS = (D_MIX // 2) // GDN_HEAD_DIM
GDN_WIDTH = GDN_HEADS * GDN_HEAD_DIM
SSM_HEAD_DIM = 64
SSM_WIDTH = D_MIX - GDN_WIDTH
SSM_HEADS = SSM_WIDTH // SSM_HEAD_DIM
SSM_GROUPS = 8
SSM_STATE = 128
SSM_GN = SSM_GROUPS * SSM_STATE
SHORT_CONV = 4
FFN_CONV = 3
D_FF = 256 * ((8 * D_MODEL // 3 + 255) // 256)
EPS = 1e-6

PROJ_SIZES = (3 * GDN_WIDTH, GDN_WIDTH, GDN_HEADS, GDN_HEADS,
              SSM_WIDTH, SSM_WIDTH + 2 * SSM_GN, SSM_HEADS)
D_IN_PROJ = sum(PROJ_SIZES)
SPLIT_POINTS = tuple(int(s) for s in np.cumsum(PROJ_SIZES)[:-1])

kernel_name = "hybrid_gdn_ssd_convffn_block"


def rmsnorm(x, w):
    xf = x.astype(jnp.float32)
    y = xf * lax.rsqrt(jnp.mean(xf * xf, axis=-1, keepdims=True) + EPS)
    return (y * w.astype(jnp.float32)).astype(x.dtype)


def causal_dwconv(x, w, b=None):
    K, C = w.shape
    y = lax.conv_general_dilated(
        x, w[:, None, :].astype(x.dtype), window_strides=(1,), padding=[(K - 1, 0)],
        dimension_numbers=("NWC", "WIO", "NWC"), feature_group_count=C)
    if b is not None:
        y = y + b.astype(x.dtype)
    return y


def l2norm(x):
    return x * lax.rsqrt(jnp.sum(x * x, axis=-1, keepdims=True) + EPS)


def gated_delta_chunked(q, k, v, g, beta):
    Bsz, L, H, Dk = q.shape
    Dv = v.shape[-1]
    N = L // CHUNK

    def to_chunks(t):
        return jnp.moveaxis(t.reshape(Bsz, N, CHUNK, H, *t.shape[3:]), 3, 1)

    q, k, v, g, beta = (to_chunks(t) for t in (q, k, v, g, beta))
    q = q * (Dk ** -0.5)
    gc = jnp.cumsum(g, axis=-1)
    idx = jnp.arange(CHUNK)
    causal = idx[:, None] >= idx[None, :]
    strict = idx[:, None] > idx[None, :]
    gamma = jnp.exp(jnp.where(causal, gc[..., :, None] - gc[..., None, :], -jnp.inf))
    kb = k * beta[..., None]
    A = jnp.where(strict, jnp.einsum("bhncd,bhnsd->bhncs", kb, k) * gamma, 0.0)
    eye = jnp.eye(CHUNK, dtype=A.dtype)
    T = lax.linalg.triangular_solve(A + eye, jnp.broadcast_to(eye, A.shape),
                                    left_side=True, lower=True, unit_diagonal=True)
    u = jnp.einsum("bhncs,bhnsv->bhncv", T, v * beta[..., None])
    w = jnp.einsum("bhncs,bhnsk->bhnck", T, kb * jnp.exp(gc)[..., None])
    attn = jnp.einsum("bhncd,bhnsd->bhncs", q, k) * gamma
    q_dec = q * jnp.exp(gc)[..., None]
    k_dec = k * jnp.exp(gc[..., -1:] - gc)[..., None]
    chunk_decay = jnp.exp(gc[..., -1])

    def step(S, inp):
        q_c, w_c, u_c, a_c, k_c, d_c = inp
        v_new = u_c - jnp.einsum("bhck,bhkv->bhcv", w_c, S)
        o = jnp.einsum("bhck,bhkv->bhcv", q_c, S) + jnp.einsum("bhcs,bhsv->bhcv", a_c, v_new)
        S = S * d_c[..., None, None] + jnp.einsum("bhck,bhcv->bhkv", k_c, v_new)
        return S, o

    xs = tuple(jnp.moveaxis(t, 2, 0) for t in (q_dec, w, u, attn, k_dec, chunk_decay))
    S0 = jnp.zeros((Bsz, H, Dk, Dv), q.dtype)
    _, o = lax.scan(step, S0, xs)
    return jnp.transpose(o, (1, 0, 3, 2, 4)).reshape(Bsz, L, H, Dv)


def ssd_chunked(x, dt, A, Bm, Cm):
    Bsz, L, H, P = x.shape
    G, Ns = Bm.shape[2], Bm.shape[3]
    R = H // G
    Nc = L // CHUNK
    x = x.reshape(Bsz, Nc, CHUNK, G, R, P)
    dt = dt.reshape(Bsz, Nc, CHUNK, G, R)
    Bm = Bm.reshape(Bsz, Nc, CHUNK, G, Ns)
    Cm = Cm.reshape(Bsz, Nc, CHUNK, G, Ns)
    acs = jnp.cumsum(dt * A.reshape(G, R), axis=2)
    idx = jnp.arange(CHUNK)
    causal = (idx[:, None] >= idx[None, :])[:, :, None, None]
    Lmat = jnp.exp(jnp.where(causal, acs[:, :, :, None] - acs[:, :, None, :], -jnp.inf))
    xdt = x * dt[..., None]
    CB = jnp.einsum("bclgn,bcsgn->bclsg", Cm, Bm)
    y_diag = jnp.einsum("bclsgr,bcsgrp->bclgrp", CB[..., None] * Lmat, xdt)
    decay_states = jnp.exp(acs[:, :, -1:] - acs)
    states = jnp.einsum("bclgn,bclgrp->bcgrpn", Bm, xdt * decay_states[..., None])
    chunk_decay = jnp.exp(acs[:, :, -1])

    def step(h, inp):
        s_c, d_c = inp
        return h * d_c[..., None, None] + s_c, h

    h0 = jnp.zeros((Bsz, G, R, P, Ns), x.dtype)
    _, prev = lax.scan(step, h0, (jnp.moveaxis(states, 1, 0), jnp.moveaxis(chunk_decay, 1, 0)))
    y_off = jnp.einsum("bclgn,cbgrpn->bclgrp", Cm, prev) * jnp.exp(acs)[..., None]
    return (y_diag + y_off).reshape(Bsz, L, H, P)


def gdn_mixer(qkv, gate, b_raw, a_raw, conv_w, A_log, dt_bias, norm_w):
    Bsz, L, _ = qkv.shape
    f32 = jnp.float32
    qkv = jax.nn.silu(causal_dwconv(qkv, conv_w)).astype(f32)
    qkv = qkv.reshape(Bsz, L, 3, GDN_HEADS, GDN_HEAD_DIM)
    q, k, v = l2norm(qkv[:, :, 0]), l2norm(qkv[:, :, 1]), qkv[:, :, 2]
    beta = jax.nn.sigmoid(b_raw.astype(f32))
    g = -jnp.exp(A_log.astype(f32)) * jax.nn.softplus(a_raw.astype(f32) + dt_bias.astype(f32))
    o = gated_delta_chunked(q, k, v, g, beta)
    o = rmsnorm(o, norm_w) * jax.nn.silu(gate.astype(f32).reshape(Bsz, L, GDN_HEADS, GDN_HEAD_DIM))
    return o.reshape(Bsz, L, GDN_WIDTH).astype(qkv.dtype)


def ssd_mixer(z, xbc, dt_raw, conv_w, conv_b, A_log, dt_bias, D_skip, norm_w):
    Bsz, L, _ = xbc.shape
    f32 = jnp.float32
    xbc = jax.nn.silu(causal_dwconv(xbc, conv_w, conv_b)).astype(f32)
    xs = xbc[..., :SSM_WIDTH].reshape(Bsz, L, SSM_HEADS, SSM_HEAD_DIM)
    Bm = xbc[..., SSM_WIDTH:SSM_WIDTH + SSM_GN].reshape(Bsz, L, SSM_GROUPS, SSM_STATE)
    Cm = xbc[..., SSM_WIDTH + SSM_GN:].reshape(Bsz, L, SSM_GROUPS, SSM_STATE)
    dt = jax.nn.softplus(dt_raw.astype(f32) + dt_bias.astype(f32))
    A = -jnp.exp(A_log.astype(f32))
    y = ssd_chunked(xs, dt, A, Bm, Cm) + xs * D_skip.astype(f32)[:, None]
    y = y.reshape(Bsz, L, SSM_WIDTH) * jax.nn.silu(z.astype(f32))
    yg = y.reshape(Bsz, L, SSM_GROUPS, SSM_WIDTH // SSM_GROUPS)
    yg = yg * lax.rsqrt(jnp.mean(yg * yg, axis=-1, keepdims=True) + EPS)
    return (yg.reshape(Bsz, L, SSM_WIDTH) * norm_w.astype(f32)).astype(z.dtype)


def conv_gated_mlp(h, w_gate, w_up, conv_w, conv_b, w_down):
    g = causal_dwconv(h @ w_gate, conv_w, conv_b)
    return (jax.nn.silu(g) * (h @ w_up)) @ w_down


def setup_inputs(seed: int = 0) -> dict:
    key = jax.random.key(seed)
    ks = jax.random.split(key, 24)
    f32 = jnp.float32

    def nrm(k, shape, scale):
        return jax.random.normal(k, shape, f32) * scale

    def gain(k, shape):
        return 1.0 + 0.05 * jax.random.normal(k, shape, f32)

    def dt_bias(k, n):
        dt = jnp.exp(jax.random.uniform(k, (DEPTH, n), f32, np.log(1e-3), np.log(1e-1)))
        return dt + jnp.log(-jnp.expm1(-dt))

    def a_log(k, n):
        return jnp.log(jax.random.uniform(k, (DEPTH, n), f32, 1.0, 16.0))

    ssm_conv_ch = SSM_WIDTH + 2 * SSM_GN
    return {
        "x": jax.random.normal(ks[0], (BATCH, SEQ, D_MODEL), f32),
        "norm1_w": gain(ks[1], (DEPTH, D_MODEL)),
        "w_in": nrm(ks[2], (DEPTH, D_MODEL, D_IN_PROJ), D_MODEL ** -0.5),
        "gdn_conv_w": nrm(ks[3], (DEPTH, SHORT_CONV, 3 * GDN_WIDTH), SHORT_CONV ** -0.5),
        "gdn_A_log": a_log(ks[4], GDN_HEADS),
        "gdn_dt_bias": dt_bias(ks[5], GDN_HEADS),
        "gdn_norm_w": gain(ks[6], (DEPTH, GDN_HEAD_DIM)),
        "ssm_conv_w": nrm(ks[7], (DEPTH, SHORT_CONV, ssm_conv_ch), SHORT_CONV ** -0.5),
        "ssm_conv_b": nrm(ks[8], (DEPTH, ssm_conv_ch), 0.02),
        "ssm_A_log": a_log(ks[9], SSM_HEADS),
        "ssm_dt_bias": dt_bias(ks[10], SSM_HEADS),
        "ssm_D": gain(ks[11], (DEPTH, SSM_HEADS)),
        "ssm_norm_w": gain(ks[12], (DEPTH, SSM_WIDTH)),
        "w_out": nrm(ks[13], (DEPTH, D_MIX, D_MODEL), D_MIX ** -0.5),
        "norm2_w": gain(ks[14], (DEPTH, D_MODEL)),
        "ffn_w_gate": nrm(ks[15], (DEPTH, D_MODEL, D_FF), D_MODEL ** -0.5),
        "ffn_w_up": nrm(ks[16], (DEPTH, D_MODEL, D_FF), D_MODEL ** -0.5),
        "ffn_conv_w": nrm(ks[17], (DEPTH, FFN_CONV, D_FF), FFN_CONV ** -0.5),
        "ffn_conv_b": nrm(ks[18], (DEPTH, D_FF), 0.02),
        "ffn_w_down": nrm(ks[19], (DEPTH, D_FF, D_MODEL), D_FF ** -0.5),
        "norm_f_w": gain(ks[20], (D_MODEL,)),
    }


def reference(x, norm1_w, w_in, gdn_conv_w, gdn_A_log, gdn_dt_bias, gdn_norm_w,
              ssm_conv_w, ssm_conv_b, ssm_A_log, ssm_dt_bias, ssm_D, ssm_norm_w,
              w_out, norm2_w, ffn_w_gate, ffn_w_up, ffn_conv_w, ffn_conv_b, ffn_w_down,
              norm_f_w):
    h = x
    for i in range(DEPTH):
        hn = rmsnorm(h, norm1_w[i])
        proj = hn @ w_in[i]
        qkv, gate, b_raw, a_raw, z, xbc, dt_raw = jnp.split(proj, SPLIT_POINTS, axis=-1)
        y_gdn = gdn_mixer(qkv, gate, b_raw, a_raw, gdn_conv_w[i], gdn_A_log[i],
                          gdn_dt_bias[i], gdn_norm_w[i])
        y_ssm = ssd_mixer(z, xbc, dt_raw, ssm_conv_w[i], ssm_conv_b[i], ssm_A_log[i],
                          ssm_dt_bias[i], ssm_D[i], ssm_norm_w[i])
        h = h + jnp.concatenate([y_gdn, y_ssm], axis=-1) @ w_out[i]
        hn = rmsnorm(h, norm2_w[i])
        h = h + conv_gated_mlp(hn, ffn_w_gate[i], ffn_w_up[i], ffn_conv_w[i],
                               ffn_conv_b[i], ffn_w_down[i])
    return rmsnorm(h, norm_f_w)
```

```python
import functools

import jax
import jax.numpy as jnp
from jax import lax
from jax.experimental import pallas as pl
from jax.experimental.pallas import tpu as pltpu

F32 = jnp.float32
BF16 = jnp.bfloat16

EPS = 1e-6
CHUNK = 64
CHUNK_SHIFT = 6
GDN_HEAD_DIM = 128
SSM_HEAD_DIM = 64
SSM_HEAD_SHIFT = 6
SSM_STATE = 128
SHORT_CONV = 4
FFN_CONV = 3
LANES = 128
SUBLANES = 8
HALO = SUBLANES
HALO_BF16 = 2 * SUBLANES
NEG_BIG = -1e30
VMEM_LIMIT = 56 * 1024 * 1024


def _pick(n, candidates):
    for c in candidates:
        if n % c == 0:
            return c
    raise ValueError(f"no tile in {candidates} divides {n}")


def _cparams(*sem):
    return pltpu.CompilerParams(dimension_semantics=sem, vmem_limit_bytes=VMEM_LIMIT)


def _silu(x):
    return x * (1.0 / (1.0 + jnp.exp(-x)))


def _dot(a, b):
    return jnp.dot(a.astype(BF16), b.astype(BF16), preferred_element_type=F32)


def _dot_nt(a, b):
    return lax.dot_general(a.astype(BF16), b.astype(BF16), (((1,), (1,)), ((), ())),
                           preferred_element_type=F32)


def _dot_tn(a, b):
    return lax.dot_general(a.astype(BF16), b.astype(BF16), (((0,), (0,)), ((), ())),
                           preferred_element_type=F32)


def _dot_f32(a, b):
    return jnp.dot(a, b, preferred_element_type=F32, precision=lax.Precision.HIGHEST)


def _rms_rows(x, w):
    return x * lax.rsqrt(jnp.mean(x * x, axis=-1, keepdims=True) + EPS) * w


def _inproj_kernel(x_ref, nw_ref, w_ref, ws_ref, o_ref, os_ref, hn_ref):
    @pl.when(pl.program_id(1) == 0)
    def _():
        hn_ref[...] = _rms_rows(x_ref[...], nw_ref[...]).astype(BF16)
        os_ref[...] = jnp.dot(hn_ref[...], ws_ref[...], preferred_element_type=F32)

    o_ref[...] = jnp.dot(hn_ref[...], w_ref[...], preferred_element_type=F32)


def _in_proj(x2, norm_w, w_main, w_small):
    T, D = x2.shape
    N = w_main.shape[1]
    tm = _pick(T, (512, 256, 128))
    tn = _pick(N, (1024, 512, 256, 128))
    return pl.pallas_call(
        _inproj_kernel,
        grid=(T // tm, N // tn),
        in_specs=[
            pl.BlockSpec((tm, D), lambda i, j: (i, 0)),
            pl.BlockSpec((1, D), lambda i, j: (0, 0)),
            pl.BlockSpec((D, tn), lambda i, j: (0, j)),
            pl.BlockSpec((D, LANES), lambda i, j: (0, 0)),
        ],
        out_specs=[
            pl.BlockSpec((tm, tn), lambda i, j: (i, j)),
            pl.BlockSpec((tm, LANES), lambda i, j: (i, 0)),
        ],
        out_shape=[jax.ShapeDtypeStruct((T, N), F32), jax.ShapeDtypeStruct((T, LANES), F32)],
        scratch_shapes=[pltpu.VMEM((tm, D), BF16)],
        compiler_params=_cparams("parallel", "arbitrary"),
        name="in_proj",
    )(x2, norm_w, w_main, w_small)


def _prep_kernel(s_ref, bias_ref, alog_ref, c1_ref, c2_ref, c2t_ref, *, n_beta, n_gdn):
    tb = s_ref.shape[0]
    s = s_ref[...]
    lane = lax.broadcasted_iota(jnp.int32, s.shape, 1)
    xb = s + bias_ref[...]
    sp = jnp.maximum(xb, 0.0) + jnp.log(1.0 + jnp.exp(-jnp.abs(xb)))
    sig = 1.0 / (1.0 + jnp.exp(-s))
    a_neg = -jnp.exp(alog_ref[...])
    step = jnp.where(lane >= n_beta, sp * a_neg, 0.0)
    row = lax.broadcasted_iota(jnp.int32, (tb, tb), 0)
    col = lax.broadcasted_iota(jnp.int32, (tb, tb), 1)
    same_chunk = jnp.right_shift(row, CHUNK_SHIFT) == jnp.right_shift(col, CHUNK_SHIFT)
    tri = jnp.where(same_chunk, jnp.where(row >= col, 1.0, 0.0), 0.0).astype(F32)
    cs = _dot_f32(tri, step)
    c1_ref[...] = jnp.where(lane < n_beta, sig, jnp.where(lane < n_gdn, cs, sp))
    c2_ref[...] = cs
    c2t_ref[...] = cs.T


def _prep(small, bias_row, alog_row, n_gdn_heads):
    T = small.shape[0]
    tb = _pick(T, (256, 128, 64))
    kern = functools.partial(_prep_kernel, n_beta=n_gdn_heads, n_gdn=2 * n_gdn_heads)
    return pl.pallas_call(
        kern,
        grid=(T // tb,),
        in_specs=[
            pl.BlockSpec((tb, LANES), lambda i: (i, 0)),
            pl.BlockSpec((1, LANES), lambda i: (0, 0)),
            pl.BlockSpec((1, LANES), lambda i: (0, 0)),
        ],
        out_specs=[
            pl.BlockSpec((tb, LANES), lambda i: (i, 0)),
            pl.BlockSpec((tb, LANES), lambda i: (i, 0)),
            pl.BlockSpec((LANES, tb), lambda i: (0, i)),
        ],
        out_shape=[
            jax.ShapeDtypeStruct((T, LANES), F32),
            jax.ShapeDtypeStruct((T, LANES), F32),
            jax.ShapeDtypeStruct((LANES, T), F32),
        ],
        compiler_params=_cparams("parallel"),
        name="prep",
    )(small, bias_row, alog_row)


def _causal_conv(x, tail_ref, idx, w, width):
    tb = x.shape[0]
    xe = jnp.concatenate([tail_ref[idx], x], axis=0)
    y = w[width - 1:width, :] * x
    for k in range(width - 1):
        off = HALO - (width - 1) + k
        y = y + w[k:k + 1, :] * xe[off:off + tb, :]
    tail_ref[idx] = x[tb - HALO:, :]
    return y


def _pick_col(cc, lane, idx):
    return jnp.sum(jnp.where(lane == idx, cc, 0.0), axis=-1, keepdims=True)


def _inv_unit_lower(a):
    n = a.shape[0]
    eye = jnp.where(lax.broadcasted_iota(jnp.int32, (n, n), 0) ==
                    lax.broadcasted_iota(jnp.int32, (n, n), 1), 1.0, 0.0).astype(F32)
    p = -a
    t = eye + p
    m = 2
    while m < n:
        p = _dot_f32(p, p)
        t = t + _dot_f32(t, p)
        m *= 2
    return t


def _gdn_kernel(q_ref, k_ref, v_ref, gate_ref, c1_ref, grow_ref, wq_ref, wk_ref, wv_ref, nw_ref,
                o_ref, state_ref, tail_ref, q_s, k_s, v_s, beta_s, gc_s, *, n_heads):
    tb = q_ref.shape[0]
    h = pl.program_id(1)

    @pl.when(pl.program_id(2) == 0)
    def _():
        state_ref[...] = jnp.zeros_like(state_ref)
        tail_ref[...] = jnp.zeros_like(tail_ref)

    def l2n(x):
        return x * lax.rsqrt(jnp.sum(x * x, axis=-1, keepdims=True) + EPS)

    q_s[...] = l2n(_silu(_causal_conv(q_ref[...], tail_ref, 0, wq_ref[...], SHORT_CONV))) * (GDN_HEAD_DIM ** -0.5)
    k_s[...] = l2n(_silu(_causal_conv(k_ref[...], tail_ref, 1, wk_ref[...], SHORT_CONV)))
    v_s[...] = _silu(_causal_conv(v_ref[...], tail_ref, 2, wv_ref[...], SHORT_CONV))

    cc = c1_ref[...]
    lane = lax.broadcasted_iota(jnp.int32, cc.shape, 1)
    beta_s[...] = jnp.broadcast_to(_pick_col(cc, lane, h), cc.shape)
    gc_s[...] = jnp.broadcast_to(_pick_col(cc, lane, n_heads + h), cc.shape)

    ri = lax.broadcasted_iota(jnp.int32, (CHUNK, CHUNK), 0)
    ci = lax.broadcasted_iota(jnp.int32, (CHUNK, CHUNK), 1)
    causal = ri >= ci
    strict = ri > ci

    def chunk(c, carry):
        r0 = pl.multiple_of(c * CHUNK, CHUNK)
        rows = pl.ds(r0, CHUNK)
        q = q_s[rows, :]
        k = k_s[rows, :]
        v = v_s[rows, :]
        beta = beta_s[rows, :]
        gcol = gc_s[rows, :]
        grow = grow_ref[0, pl.ds(c, 1), :]
        gamma = jnp.exp(jnp.where(causal, gcol[:, :CHUNK] - grow, NEG_BIG))
        kb = k * beta
        a = jnp.where(strict, _dot_nt(kb, k) * gamma, 0.0)
        t_inv = _inv_unit_lower(a)
        eg = jnp.exp(gcol)
        u = _dot(t_inv, v * beta)
        w = _dot(t_inv, kb * eg)
        attn = _dot_nt(q, k) * gamma
        glast = gcol[CHUNK - 1:CHUNK, :]
        k_dec = k * jnp.exp(glast - gcol)
        s = state_ref[...]
        v_new = u - _dot(w, s)
        o = _dot(q * eg, s) + _dot(attn, v_new)
        state_ref[...] = s * jnp.exp(glast) + _dot_tn(k_dec, v_new)
        o = _rms_rows(o, nw_ref[...]) * _silu(gate_ref[rows, :])
        o_ref[rows, :] = o.astype(o_ref.dtype)
        return carry

    lax.fori_loop(0, tb // CHUNK, chunk, 0)


def _gdn(proj, c1, c2t3, conv_w, norm_w, *, batch, seq, n_heads, gate_col0):
    T = proj.shape[0]
    tb = _pick(seq, (512, 256, 128, 64))
    nt = seq // tb
    hd = GDN_HEAD_DIM
    cpb = tb // CHUNK

    def rows(b, h, t):
        return b * nt + t

    kern = functools.partial(_gdn_kernel, n_heads=n_heads)
    return pl.pallas_call(
        kern,
        grid=(batch, n_heads, nt),
        in_specs=[
            pl.BlockSpec((tb, hd), lambda b, h, t: (rows(b, h, t), h)),
            pl.BlockSpec((tb, hd), lambda b, h, t: (rows(b, h, t), n_heads + h)),
            pl.BlockSpec((tb, hd), lambda b, h, t: (rows(b, h, t), 2 * n_heads + h)),
            pl.BlockSpec((tb, hd), lambda b, h, t: (rows(b, h, t), gate_col0 + h)),
            pl.BlockSpec((tb, LANES), lambda b, h, t: (rows(b, h, t), 0)),
            pl.BlockSpec((1, cpb, CHUNK), lambda b, h, t: (n_heads + h, rows(b, h, t), 0)),
            pl.BlockSpec((SHORT_CONV, hd), lambda b, h, t: (0, h)),
            pl.BlockSpec((SHORT_CONV, hd), lambda b, h, t: (0, n_heads + h)),
            pl.BlockSpec((SHORT_CONV, hd), lambda b, h, t: (0, 2 * n_heads + h)),
            pl.BlockSpec((1, hd), lambda b, h, t: (0, 0)),
        ],
        out_specs=pl.BlockSpec((tb, hd), lambda b, h, t: (rows(b, h, t), h)),
        out_shape=jax.ShapeDtypeStruct((T, n_heads * hd), BF16),
        scratch_shapes=[
            pltpu.VMEM((hd, hd), F32),
            pltpu.VMEM((3, HALO, hd), F32),
            pltpu.VMEM((tb, hd), F32),
            pltpu.VMEM((tb, hd), F32),
            pltpu.VMEM((tb, hd), F32),
            pltpu.VMEM((tb, LANES), F32),
            pltpu.VMEM((tb, LANES), F32),
        ],
        compiler_params=_cparams("parallel", "parallel", "arbitrary"),
        name="gdn",
    )(proj, proj, proj, proj, c1, c2t3, conv_w, conv_w, conv_w, norm_w)


def _ssd_kernel(x_ref, bm_ref, cm_ref, z_ref, c1_ref, c2_ref, arow_ref,
                wx_ref, wb_ref, wc_ref, bx_ref, bb_ref, bc_ref, dskip_ref, nw_ref,
                o_ref, state_ref, tail_ref, x_s, bm_s, cm_s, dt_s, ax_s, acol_s,
                *, dt_col0, heads_per_group):
    tb, width = x_ref.shape
    R = heads_per_group
    P = SSM_HEAD_DIM
    g = pl.program_id(1)

    @pl.when(pl.program_id(2) == 0)
    def _():
        state_ref[...] = jnp.zeros_like(state_ref)
        tail_ref[...] = jnp.zeros_like(tail_ref)

    def conv(ref, lo, hi, w_ref, b_ref):
        x = ref[...]
        xe = jnp.concatenate([tail_ref[:, lo:hi], x], axis=0)
        w = w_ref[...]
        y = w[SHORT_CONV - 1:SHORT_CONV, :] * x
        for k in range(SHORT_CONV - 1):
            off = HALO - (SHORT_CONV - 1) + k
            y = y + w[k:k + 1, :] * xe[off:off + tb, :]
        tail_ref[:, lo:hi] = x[tb - HALO:, :]
        return _silu(y + b_ref[...])

    x_s[...] = conv(x_ref, 0, width, wx_ref, bx_ref)
    bm_s[...] = conv(bm_ref, width, width + SSM_STATE, wb_ref, bb_ref)
    cm_s[...] = conv(cm_ref, width + SSM_STATE, width + 2 * SSM_STATE, wc_ref, bc_ref)

    c1 = c1_ref[...]
    c2 = c2_ref[...]
    lane = lax.broadcasted_iota(jnp.int32, c1.shape, 1)
    grp = jnp.right_shift(lax.broadcasted_iota(jnp.int32, (tb, width), 1), SSM_HEAD_SHIFT)
    dtx = jnp.zeros((tb, width), F32)
    ax = jnp.zeros((tb, width), F32)
    for r in range(R):
        idx = dt_col0 + g * R + r
        dcol = _pick_col(c1, lane, idx)
        acol = _pick_col(c2, lane, idx)
        dtx = jnp.where(grp == r, dcol, dtx)
        ax = jnp.where(grp == r, acol, ax)
        acol_s[r] = jnp.broadcast_to(acol, (tb, LANES))
    dt_s[...] = dtx
    ax_s[...] = ax

    ri = lax.broadcasted_iota(jnp.int32, (CHUNK, CHUNK), 0)
    ci = lax.broadcasted_iota(jnp.int32, (CHUNK, CHUNK), 1)
    causal = ri >= ci
    grp_c = jnp.right_shift(lax.broadcasted_iota(jnp.int32, (CHUNK, width), 1), SSM_HEAD_SHIFT)

    def chunk(c, carry):
        r0 = pl.multiple_of(c * CHUNK, CHUNK)
        rows = pl.ds(r0, CHUNK)
        x = x_s[rows, :]
        bm = bm_s[rows, :]
        cm = cm_s[rows, :]
        a = ax_s[rows, :]
        xdt = x * dt_s[rows, :]
        cb = _dot_nt(cm, bm)
        ydiag = jnp.zeros((CHUNK, width), F32)
        for r in range(R):
            arow = arow_ref[r, pl.ds(c, 1), :]
            acol = acol_s[r, rows, :][:, :CHUNK]
            lmat = jnp.exp(jnp.where(causal, acol - arow, NEG_BIG))
            ydiag = ydiag + _dot(cb * lmat, jnp.where(grp_c == r, xdt, 0.0))
        alast = a[CHUNK - 1:CHUNK, :]
        hstate = state_ref[...]
        yoff = _dot(cm, hstate) * jnp.exp(a)
        state_ref[...] = hstate * jnp.exp(alast) + _dot_tn(bm, xdt * jnp.exp(alast - a))
        y = ydiag + yoff + x * dskip_ref[...]
        y = y * _silu(z_ref[rows, :])
        o_ref[rows, :] = _rms_rows(y, nw_ref[...]).astype(o_ref.dtype)
        return carry

    lax.fori_loop(0, tb // CHUNK, chunk, 0)


def _ssd(proj, c1, c2, c2t3, conv_w, conv_b, dskip_x, norm_w, *, batch, seq, n_groups,
         heads_per_group, z_col0, x_col0, dt_col0):
    T = proj.shape[0]
    R = heads_per_group
    width = R * SSM_HEAD_DIM
    ssm_width = n_groups * width
    ns = SSM_STATE
    tb = _pick(seq, (512, 256, 128, 64))
    nt = seq // tb
    cpb = tb // CHUNK
    assert x_col0 % width == 0 and z_col0 % width == 0 and dt_col0 % R == 0
    assert (x_col0 + ssm_width) % ns == 0 and ssm_width % ns == 0
    xb0 = x_col0 // width
    zb0 = z_col0 // width
    bb0 = (x_col0 + ssm_width) // ns
    cb0 = bb0 + n_groups
    wb0 = ssm_width // ns

    def rows(b, g, t):
        return b * nt + t

    kern = functools.partial(_ssd_kernel, dt_col0=dt_col0, heads_per_group=R)
    return pl.pallas_call(
        kern,
        grid=(batch, n_groups, nt),
        in_specs=[
            pl.BlockSpec((tb, width), lambda b, g, t: (rows(b, g, t), xb0 + g)),
            pl.BlockSpec((tb, ns), lambda b, g, t: (rows(b, g, t), bb0 + g)),
            pl.BlockSpec((tb, ns), lambda b, g, t: (rows(b, g, t), cb0 + g)),
            pl.BlockSpec((tb, width), lambda b, g, t: (rows(b, g, t), zb0 + g)),
            pl.BlockSpec((tb, LANES), lambda b, g, t: (rows(b, g, t), 0)),
            pl.BlockSpec((tb, LANES), lambda b, g, t: (rows(b, g, t), 0)),
            pl.BlockSpec((R, cpb, CHUNK), lambda b, g, t: (dt_col0 // R + g, rows(b, g, t), 0)),
            pl.BlockSpec((SHORT_CONV, width), lambda b, g, t: (0, g)),
            pl.BlockSpec((SHORT_CONV, ns), lambda b, g, t: (0, wb0 + g)),
            pl.BlockSpec((SHORT_CONV, ns), lambda b, g, t: (0, wb0 + n_groups + g)),
            pl.BlockSpec((1, width), lambda b, g, t: (0, g)),
            pl.BlockSpec((1, ns), lambda b, g, t: (0, wb0 + g)),
            pl.BlockSpec((1, ns), lambda b, g, t: (0, wb0 + n_groups + g)),
            pl.BlockSpec((1, width), lambda b, g, t: (0, g)),
            pl.BlockSpec((1, width), lambda b, g, t: (0, g)),
        ],
        out_specs=pl.BlockSpec((tb, width), lambda b, g, t: (rows(b, g, t), g)),
        out_shape=jax.ShapeDtypeStruct((T, ssm_width), BF16),
        scratch_shapes=[
            pltpu.VMEM((ns, width), F32),
            pltpu.VMEM((HALO, width + 2 * ns), F32),
            pltpu.VMEM((tb, width), F32),
            pltpu.VMEM((tb, ns), F32),
            pltpu.VMEM((tb, ns), F32),
            pltpu.VMEM((tb, width), F32),
            pltpu.VMEM((tb, width), F32),
            pltpu.VMEM((R, tb, LANES), F32),
        ],
        compiler_params=_cparams("parallel", "parallel", "arbitrary"),
        name="ssd",
    )(proj, proj, proj, proj, c1, c2, c2t3, conv_w, conv_w, conv_w, conv_b, conv_b, conv_b,
      dskip_x, norm_w)


def _outproj_kernel(x_ref, yg_ref, ys_ref, wg_ref, ws_ref, o_ref):
    acc = jnp.dot(yg_ref[...], wg_ref[...], preferred_element_type=F32)
    acc = acc + jnp.dot(ys_ref[...], ws_ref[...], preferred_element_type=F32)
    o_ref[...] = x_ref[...] + acc


def _out_proj(x2, y_gdn, y_ssm, w_g, w_s):
    T, D = x2.shape
    kg, ks = y_gdn.shape[1], y_ssm.shape[1]
    tm = _pick(T, (512, 256, 128))
    tn = _pick(D, (1024, 512, 256, 128))
    return pl.pallas_call(
        _outproj_kernel,
        grid=(T // tm, D // tn),
        in_specs=[
            pl.BlockSpec((tm, tn), lambda i, j: (i, j)),
            pl.BlockSpec((tm, kg), lambda i, j: (i, 0)),
            pl.BlockSpec((tm, ks), lambda i, j: (i, 0)),
            pl.BlockSpec((kg, tn), lambda i, j: (0, j)),
            pl.BlockSpec((ks, tn), lambda i, j: (0, j)),
        ],
        out_specs=pl.BlockSpec((tm, tn), lambda i, j: (i, j)),
        out_shape=jax.ShapeDtypeStruct((T, D), F32),
        compiler_params=_cparams("parallel", "arbitrary"),
        name="out_proj",
    )(x2, y_gdn, y_ssm, w_g, w_s)


def _ffnup_kernel(h_ref, halo_ref, nw_ref, wg_ref, wu_ref, cw_ref, cb_ref, o_ref, hn_ref,
                  *, blocks_per_seq):
    tm = h_ref.shape[0]
    i = pl.program_id(0)

    @pl.when(pl.program_id(1) == 0)
    def _():
        nw = nw_ref[...]
        keep = jnp.where(i % blocks_per_seq == 0, 0.0, 1.0)
        hn_ref[0:HALO_BF16, :] = (_rms_rows(halo_ref[...], nw) * keep).astype(BF16)
        hn_ref[HALO_BF16:, :] = _rms_rows(h_ref[...], nw).astype(BF16)

    hn = hn_ref[...]
    ge = jnp.dot(hn, wg_ref[...], preferred_element_type=F32)
    up = jnp.dot(hn_ref[HALO_BF16:, :], wu_ref[...], preferred_element_type=F32)
    cw = cw_ref[...]
    g = cb_ref[...] + cw[FFN_CONV - 1:FFN_CONV, :] * ge[HALO_BF16:, :]
    for k in range(FFN_CONV - 1):
        off = HALO_BF16 - (FFN_CONV - 1) + k
        g = g + cw[k:k + 1, :] * ge[off:off + tm, :]
    o_ref[...] = (_silu(g) * up).astype(o_ref.dtype)


def _ffn_up(h2, norm_w, w_gate, w_up, conv_w, conv_b, *, seq):
    T, D = h2.shape
    F = w_gate.shape[1]
    tm = _pick(seq, (512, 256, 128))
    tn = _pick(F, (512, 256, 128))
    kern = functools.partial(_ffnup_kernel, blocks_per_seq=seq // tm)
    hb = tm // HALO_BF16
    return pl.pallas_call(
        kern,
        grid=(T // tm, F // tn),
        in_specs=[
            pl.BlockSpec((tm, D), lambda i, j: (i, 0)),
            pl.BlockSpec((HALO_BF16, D), lambda i, j: (jnp.maximum(i * hb - 1, 0), 0)),
            pl.BlockSpec((1, D), lambda i, j: (0, 0)),
            pl.BlockSpec((D, tn), lambda i, j: (0, j)),
            pl.BlockSpec((D, tn), lambda i, j: (0, j)),
            pl.BlockSpec((FFN_CONV, tn), lambda i, j: (0, j)),
            pl.BlockSpec((1, tn), lambda i, j: (0, j)),
        ],
        out_specs=pl.BlockSpec((tm, tn), lambda i, j: (i, j)),
        out_shape=jax.ShapeDtypeStruct((T, F), BF16),
        scratch_shapes=[pltpu.VMEM((HALO_BF16 + tm, D), BF16)],
        compiler_params=_cparams("parallel", "arbitrary"),
        name="ffn_up",
    )(h2, h2, norm_w, w_gate, w_up, conv_w, conv_b)


def _ffndown_kernel(a_ref, w_ref, h_ref, nw_ref, o_ref, *, final_norm):
    k = pl.program_id(1)

    @pl.when(k == 0)
    def _():
        o_ref[...] = h_ref[...]

    o_ref[...] += jnp.dot(a_ref[...], w_ref[...], preferred_element_type=F32)

    if final_norm:
        @pl.when(k == pl.num_programs(1) - 1)
        def _():
            o_ref[...] = _rms_rows(o_ref[...], nw_ref[...])


def _ffn_down(act, w_down, h2, norm_w, *, final_norm):
    T, F = act.shape
    D = w_down.shape[1]
    tm = _pick(T, (512, 256, 128))
    tk = _pick(F, (512, 256, 128))
    return pl.pallas_call(
        functools.partial(_ffndown_kernel, final_norm=final_norm),
        grid=(T // tm, F // tk),
        in_specs=[
            pl.BlockSpec((tm, tk), lambda i, k: (i, k)),
            pl.BlockSpec((tk, D), lambda i, k: (k, 0)),
            pl.BlockSpec((tm, D), lambda i, k: (i, 0)),
            pl.BlockSpec((1, D), lambda i, k: (0, 0)),
        ],
        out_specs=pl.BlockSpec((tm, D), lambda i, k: (i, 0)),
        out_shape=jax.ShapeDtypeStruct((T, D), F32),
        compiler_params=_cparams("parallel", "arbitrary"),
        name="ffn_down",
    )(act, w_down, h2, norm_w)


def _pad_to(a, axis, size):
    pad = size - a.shape[axis]
    if pad == 0:
        return a
    widths = [(0, 0)] * a.ndim
    widths[axis] = (0, pad)
    return jnp.pad(a, widths)


def kernel(x, norm1_w, w_in, gdn_conv_w, gdn_A_log, gdn_dt_bias, gdn_norm_w, ssm_conv_w, ssm_conv_b,
           ssm_A_log, ssm_dt_bias, ssm_D, ssm_norm_w, w_out, norm2_w, ffn_w_gate, ffn_w_up,
           ffn_conv_w, ffn_conv_b, ffn_w_down, norm_f_w):
    B, L, D = x.shape
    depth = w_in.shape[0]
    gh = gdn_A_log.shape[1]
    sh = ssm_A_log.shape[1]
    gw = gh * GDN_HEAD_DIM
    sw = sh * SSM_HEAD_DIM
    gn = (ssm_conv_w.shape[-1] - sw) // 2
    n_groups = gn // SSM_STATE
    hpg = sh // n_groups
    assert 2 * gh + sh <= LANES and L % CHUNK == 0
    T = B * L
    f_dim = ffn_w_gate.shape[-1]
    f_pad = -(-f_dim // 1024) * 1024 if f_dim >= 1024 else f_dim

    o_gate = 3 * gw
    o_b = o_gate + gw
    o_a = o_b + gh
    o_z = o_a + gh
    o_xbc = o_z + sw
    o_dt = o_xbc + sw + 2 * gn

    h2 = x.reshape(T, D)
    for i in range(depth):
        wi = w_in[i]
        w_main = jnp.concatenate([wi[:, :o_b], wi[:, o_z:o_dt]], axis=1).astype(BF16)
        w_small = _pad_to(jnp.concatenate([wi[:, o_b:o_z], wi[:, o_dt:]], axis=1), 1, LANES).astype(BF16)
        z_col0 = o_b
        x_col0 = o_b + sw

        proj, small = _in_proj(h2, norm1_w[i][None, :], w_main, w_small)

        bias_row = _pad_to(jnp.concatenate([jnp.zeros((gh,), F32), gdn_dt_bias[i], ssm_dt_bias[i]])[None, :], 1, LANES)
        alog_row = _pad_to(jnp.concatenate([jnp.zeros((gh,), F32), gdn_A_log[i], ssm_A_log[i]])[None, :], 1, LANES)
        c1, c2, c2t = _prep(small, bias_row, alog_row, gh)
        c2t3 = c2t.reshape(LANES, T // CHUNK, CHUNK)

        y_gdn = _gdn(proj, c1, c2t3, gdn_conv_w[i], gdn_norm_w[i][None, :],
                     batch=B, seq=L, n_heads=gh, gate_col0=o_gate // GDN_HEAD_DIM)
        y_ssm = _ssd(proj, c1, c2, c2t3, ssm_conv_w[i], ssm_conv_b[i][None, :],
                     jnp.repeat(ssm_D[i], SSM_HEAD_DIM)[None, :], ssm_norm_w[i][None, :],
                     batch=B, seq=L, n_groups=n_groups, heads_per_group=hpg,
                     z_col0=z_col0, x_col0=x_col0, dt_col0=2 * gh)

        wo = w_out[i].astype(BF16)
        h2 = _out_proj(h2, y_gdn, y_ssm, wo[:gw], wo[gw:])

        act = _ffn_up(h2, norm2_w[i][None, :],
                      _pad_to(ffn_w_gate[i], 1, f_pad).astype(BF16),
                      _pad_to(ffn_w_up[i], 1, f_pad).astype(BF16),
                      _pad_to(ffn_conv_w[i], 1, f_pad), _pad_to(ffn_conv_b[i][None, :], 1, f_pad), seq=L)
        h2 = _ffn_down(act, _pad_to(ffn_w_down[i], 0, f_pad).astype(BF16), h2, norm_f_w[None, :],
                       final_norm=(i == depth - 1))
    return h2.reshape(B, L, D)
```

```python
import functools

import jax
import jax.numpy as jnp
from jax import lax
from jax.experimental import pallas as pl
from jax.experimental.pallas import tpu as pltpu

F32 = jnp.float32
BF16 = jnp.bfloat16

EPS = 1e-6
CHUNK = 64
CHUNK_SHIFT = 6
GDN_HEAD_DIM = 128
GDN_HEADS_PER_STEP = 4
GDN_CHUNKS_PER_ITER = 4
INV_BASE = 8
SSM_HEAD_DIM = 64
SSM_HEAD_SHIFT = 6
SSM_STATE = 128
SHORT_CONV = 4
FFN_CONV = 3
LANES = 128
SUBLANES = 8
HALO = SUBLANES
HALO_BF16 = 2 * SUBLANES
NEG_BIG = -1e30
VMEM_LIMIT = 56 * 1024 * 1024


def _pick(n, candidates):
    for c in candidates:
        if n % c == 0:
            return c
    raise ValueError(f"no tile in {candidates} divides {n}")


def _cparams(*sem):
    return pltpu.CompilerParams(dimension_semantics=sem, vmem_limit_bytes=VMEM_LIMIT)


def _silu(x):
    return x * (1.0 / (1.0 + jnp.exp(-x)))


def _dot(a, b):
    return jnp.dot(a.astype(BF16), b.astype(BF16), preferred_element_type=F32)


def _dot_nt(a, b):
    return lax.dot_general(a.astype(BF16), b.astype(BF16), (((1,), (1,)), ((), ())),
                           preferred_element_type=F32)


def _dot_tn(a, b):
    return lax.dot_general(a.astype(BF16), b.astype(BF16), (((0,), (0,)), ((), ())),
                           preferred_element_type=F32)


def _dot_f32(a, b):
    return jnp.dot(a, b, preferred_element_type=F32, precision=lax.Precision.HIGHEST)


def _rms_rows(x, w):
    return x * lax.rsqrt(jnp.mean(x * x, axis=-1, keepdims=True) + EPS) * w


def _inproj_kernel(x_ref, nw_ref, w_ref, ws_ref, o_ref, os_ref, hn_ref):
    @pl.when(pl.program_id(1) == 0)
    def _():
        hn_ref[...] = _rms_rows(x_ref[...], nw_ref[...]).astype(BF16)
        os_ref[...] = jnp.dot(hn_ref[...], ws_ref[...], preferred_element_type=F32)

    o_ref[...] = jnp.dot(hn_ref[...], w_ref[...], preferred_element_type=F32)


def _in_proj(x2, norm_w, w_main, w_small):
    T, D = x2.shape
    N = w_main.shape[1]
    tm = _pick(T, (512, 256, 128))
    tn = _pick(N, (1024, 512, 256, 128))
    return pl.pallas_call(
        _inproj_kernel,
        grid=(T // tm, N // tn),
        in_specs=[
            pl.BlockSpec((tm, D), lambda i, j: (i, 0)),
            pl.BlockSpec((1, D), lambda i, j: (0, 0)),
            pl.BlockSpec((D, tn), lambda i, j: (0, j)),
            pl.BlockSpec((D, LANES), lambda i, j: (0, 0)),
        ],
        out_specs=[
            pl.BlockSpec((tm, tn), lambda i, j: (i, j)),
            pl.BlockSpec((tm, LANES), lambda i, j: (i, 0)),
        ],
        out_shape=[jax.ShapeDtypeStruct((T, N), F32), jax.ShapeDtypeStruct((T, LANES), F32)],
        scratch_shapes=[pltpu.VMEM((tm, D), BF16)],
        compiler_params=_cparams("parallel", "arbitrary"),
        name="in_proj",
    )(x2, norm_w, w_main, w_small)


def _prep_kernel(s_ref, bias_ref, alog_ref, c1_ref, c2_ref, c2t_ref, *, n_beta, n_gdn):
    tb = s_ref.shape[0]
    s = s_ref[...]
    lane = lax.broadcasted_iota(jnp.int32, s.shape, 1)
    xb = s + bias_ref[...]
    sp = jnp.maximum(xb, 0.0) + jnp.log(1.0 + jnp.exp(-jnp.abs(xb)))
    sig = 1.0 / (1.0 + jnp.exp(-s))
    a_neg = -jnp.exp(alog_ref[...])
    step = jnp.where(lane >= n_beta, sp * a_neg, 0.0)
    row = lax.broadcasted_iota(jnp.int32, (tb, tb), 0)
    col = lax.broadcasted_iota(jnp.int32, (tb, tb), 1)
    same_chunk = jnp.right_shift(row, CHUNK_SHIFT) == jnp.right_shift(col, CHUNK_SHIFT)
    tri = jnp.where(same_chunk, jnp.where(row >= col, 1.0, 0.0), 0.0).astype(F32)
    cs = _dot_f32(tri, step)
    c1_ref[...] = jnp.where(lane < n_beta, sig, jnp.where(lane < n_gdn, cs, sp))
    c2_ref[...] = cs
    c2t_ref[...] = cs.T


def _prep(small, bias_row, alog_row, n_gdn_heads):
    T = small.shape[0]
    tb = _pick(T, (256, 128, 64))
    kern = functools.partial(_prep_kernel, n_beta=n_gdn_heads, n_gdn=2 * n_gdn_heads)
    return pl.pallas_call(
        kern,
        grid=(T // tb,),
        in_specs=[
            pl.BlockSpec((tb, LANES), lambda i: (i, 0)),
            pl.BlockSpec((1, LANES), lambda i: (0, 0)),
            pl.BlockSpec((1, LANES), lambda i: (0, 0)),
        ],
        out_specs=[
            pl.BlockSpec((tb, LANES), lambda i: (i, 0)),
            pl.BlockSpec((tb, LANES), lambda i: (i, 0)),
            pl.BlockSpec((LANES, tb), lambda i: (0, i)),
        ],
        out_shape=[
            jax.ShapeDtypeStruct((T, LANES), F32),
            jax.ShapeDtypeStruct((T, LANES), F32),
            jax.ShapeDtypeStruct((LANES, T), F32),
        ],
        compiler_params=_cparams("parallel"),
        name="prep",
    )(small, bias_row, alog_row)


def _causal_conv(x, tail_ref, idx, w, width):
    tb = x.shape[0]
    xe = jnp.concatenate([tail_ref[idx], x], axis=0)
    y = w[width - 1:width, :] * x
    for k in range(width - 1):
        off = HALO - (width - 1) + k
        y = y + w[k:k + 1, :] * xe[off:off + tb, :]
    tail_ref[idx] = x[tb - HALO:, :]
    return y


def _pick_col(cc, lane, idx):
    return jnp.sum(jnp.where(lane == idx, cc, 0.0), axis=-1, keepdims=True)


def _inv_unit_lower(a):
    return _inv_unit_lower_many([a])[0]


def _each(f, *lists):
    return [f(*xs) for xs in zip(*lists)]


def _inv_unit_lower_many(mats):
    n = mats[0].shape[0]
    ri = lax.broadcasted_iota(jnp.int32, (n, n), 0)
    ci = lax.broadcasted_iota(jnp.int32, (n, n), 1)
    eye = jnp.where(ri == ci, 1.0, 0.0).astype(F32)
    blk = ri ^ ci
    ps = _each(lambda a: jnp.where(blk < INV_BASE, -a, 0.0), mats)
    ts = _each(lambda p: eye + p, ps)
    m = 2
    while m < INV_BASE:
        ps = _each(lambda p: _dot(p, p), ps)
        ts = _each(lambda t, p: t + _dot(t, p), ts, ps)
        m *= 2
    m = INV_BASE
    while m < n:
        offs = _each(lambda a: jnp.where(blk >= m, jnp.where(blk < 2 * m, a, 0.0), 0.0), mats)
        xs = _each(_dot, offs, ts)
        ts = _each(lambda t, x: t - _dot(t, x), ts, xs)
        m *= 2
    return ts


def _gdn_kernel(q_ref, k_ref, v_ref, gate_ref, c1_ref, grow_ref, wq_ref, wk_ref, wv_ref, nw_ref,
                o_ref, state_ref, tail_ref, q_s, k_s, v_s, u_s, w_s, qd_s, kd_s, attn_s, beta_s, gc_s,
                *, n_heads, hb, chunks_per_iter):
    tb = q_ref.shape[0]
    hd = GDN_HEAD_DIM
    h0 = pl.program_id(1) * hb
    n_chunks = tb // CHUNK

    @pl.when(pl.program_id(2) == 0)
    def _():
        state_ref[...] = jnp.zeros_like(state_ref)
        tail_ref[...] = jnp.zeros_like(tail_ref)

    def l2n(x):
        return x * lax.rsqrt(jnp.sum(x * x, axis=-1, keepdims=True) + EPS)

    qc = _silu(_causal_conv(q_ref[...], tail_ref, 0, wq_ref[...], SHORT_CONV))
    kc = _silu(_causal_conv(k_ref[...], tail_ref, 1, wk_ref[...], SHORT_CONV))
    v_s[...] = _silu(_causal_conv(v_ref[...], tail_ref, 2, wv_ref[...], SHORT_CONV))
    cc = c1_ref[...]
    lane = lax.broadcasted_iota(jnp.int32, cc.shape, 1)
    for hh in range(hb):
        cols = slice(hh * hd, (hh + 1) * hd)
        q_s[:, cols] = l2n(qc[:, cols]) * (hd ** -0.5)
        k_s[:, cols] = l2n(kc[:, cols])
        beta_s[hh] = jnp.broadcast_to(_pick_col(cc, lane, h0 + hh), cc.shape)
        gc_s[hh] = jnp.broadcast_to(_pick_col(cc, lane, n_heads + h0 + hh), cc.shape)

    ri = lax.broadcasted_iota(jnp.int32, (CHUNK, CHUNK), 0)
    ci = lax.broadcasted_iota(jnp.int32, (CHUNK, CHUNK), 1)
    causal = ri >= ci
    strict = ri > ci

    def intra_load(c, hh):
        rows = pl.ds(pl.multiple_of(c * CHUNK, CHUNK), CHUNK)
        cols = slice(hh * hd, (hh + 1) * hd)
        return (q_s[rows, cols], k_s[rows, cols], v_s[rows, cols], beta_s[hh, rows, :],
                gc_s[hh, rows, :], grow_ref[hh, pl.ds(c, 1), :])

    def intra_compute(q, k, v, beta, gcol, grow):
        gamma = _each(lambda gc, gr: jnp.exp(jnp.where(causal, gc[:, :CHUNK] - gr, NEG_BIG)), gcol, grow)
        kb = _each(jnp.multiply, k, beta)
        a = _each(lambda kb_, k_, gm: jnp.where(strict, _dot_nt(kb_, k_) * gm, 0.0), kb, k, gamma)
        attn = _each(lambda q_, k_, gm: _dot_nt(q_, k_) * gm, q, k, gamma)
        t_inv = _inv_unit_lower_many(a)
        eg = _each(jnp.exp, gcol)
        u = _each(lambda t, v_, b: _dot(t, v_ * b), t_inv, v, beta)
        w = _each(lambda t, kb_, e: _dot(t, kb_ * e), t_inv, kb, eg)
        q_dec = _each(jnp.multiply, q, eg)
        k_dec = _each(lambda k_, gc: k_ * jnp.exp(gc[CHUNK - 1:CHUNK, :] - gc), k, gcol)
        return list(zip(u, w, attn, q_dec, k_dec))

    def intra_store(c, hh, u, w, attn, q_dec, k_dec):
        rows = pl.ds(pl.multiple_of(c * CHUNK, CHUNK), CHUNK)
        cols = slice(hh * hd, (hh + 1) * hd)
        u_s[rows, cols] = u
        w_s[rows, cols] = w
        attn_s[hh, rows, :] = attn
        qd_s[rows, cols] = q_dec
        kd_s[rows, cols] = k_dec

    heads = list(range(hb))
    head_cols = [slice(hh * hd, (hh + 1) * hd) for hh in heads]

    def inter(c):
        r0 = pl.multiple_of(c * CHUNK, CHUNK)
        rows = pl.ds(r0, CHUNK)
        s = [state_ref[hh] for hh in heads]
        u = [u_s[rows, cols] for cols in head_cols]
        w = [w_s[rows, cols] for cols in head_cols]
        qd = [qd_s[rows, cols] for cols in head_cols]
        kd = [kd_s[rows, cols] for cols in head_cols]
        attn = [attn_s[hh, rows, :] for hh in heads]
        dec = [jnp.exp(gc_s[hh, pl.ds(r0 + CHUNK - 1, 1), :]) for hh in heads]
        gate = [gate_ref[rows, cols] for cols in head_cols]
        ws = _each(_dot, w, s)
        qs = _each(_dot, qd, s)
        v_new = _each(jnp.subtract, u, ws)
        kv = _each(_dot_tn, kd, v_new)
        av = _each(_dot, attn, v_new)
        for hh in heads:
            state_ref[hh] = s[hh] * dec[hh] + kv[hh]
        nw = nw_ref[...]
        for hh in heads:
            o = _rms_rows(qs[hh] + av[hh], nw) * _silu(gate[hh])
            o_ref[rows, head_cols[hh]] = o.astype(o_ref.dtype)

    def pass1(it, carry):
        work = [(it * chunks_per_iter + cc_, hh) for cc_ in range(chunks_per_iter) for hh in range(hb)]
        loaded = [intra_load(c, hh) for c, hh in work]
        results = intra_compute(*[list(col) for col in zip(*loaded)])
        for (c, hh), res in zip(work, results):
            intra_store(c, hh, *res)
        return carry

    def pass2(c, carry):
        inter(c)
        return carry

    lax.fori_loop(0, n_chunks // chunks_per_iter, pass1, 0)
    lax.fori_loop(0, n_chunks, pass2, 0)


def _gdn(proj, c1, c2t3, conv_w, norm_w, *, batch, seq, n_heads, gate_col0):
    T = proj.shape[0]
    tb = _pick(seq, (512, 256, 128, 64))
    nt = seq // tb
    hd = GDN_HEAD_DIM
    cpb = tb // CHUNK
    hb = _pick(n_heads, (GDN_HEADS_PER_STEP, 1))
    cpi = _pick(cpb, (GDN_CHUNKS_PER_ITER, 1))
    bw = hb * hd
    ng = n_heads // hb

    def rows(b, h, t):
        return b * nt + t

    kern = functools.partial(_gdn_kernel, n_heads=n_heads, hb=hb, chunks_per_iter=cpi)
    return pl.pallas_call(
        kern,
        grid=(batch, ng, nt),
        in_specs=[
            pl.BlockSpec((tb, bw), lambda b, h, t: (rows(b, h, t), h)),
            pl.BlockSpec((tb, bw), lambda b, h, t: (rows(b, h, t), ng + h)),
            pl.BlockSpec((tb, bw), lambda b, h, t: (rows(b, h, t), 2 * ng + h)),
            pl.BlockSpec((tb, bw), lambda b, h, t: (rows(b, h, t), gate_col0 // hb + h)),
            pl.BlockSpec((tb, LANES), lambda b, h, t: (rows(b, h, t), 0)),
            pl.BlockSpec((hb, cpb, CHUNK), lambda b, h, t: (ng + h, rows(b, h, t), 0)),
            pl.BlockSpec((SHORT_CONV, bw), lambda b, h, t: (0, h)),
            pl.BlockSpec((SHORT_CONV, bw), lambda b, h, t: (0, ng + h)),
            pl.BlockSpec((SHORT_CONV, bw), lambda b, h, t: (0, 2 * ng + h)),
            pl.BlockSpec((1, hd), lambda b, h, t: (0, 0)),
        ],
        out_specs=pl.BlockSpec((tb, bw), lambda b, h, t: (rows(b, h, t), h)),
        out_shape=jax.ShapeDtypeStruct((T, n_heads * hd), BF16),
        scratch_shapes=[
            pltpu.VMEM((hb, hd, hd), F32),
            pltpu.VMEM((3, HALO, bw), F32),
            pltpu.VMEM((tb, bw), F32),
            pltpu.VMEM((tb, bw), F32),
            pltpu.VMEM((tb, bw), F32),
            pltpu.VMEM((tb, bw), F32),
            pltpu.VMEM((tb, bw), F32),
            pltpu.VMEM((tb, bw), F32),
            pltpu.VMEM((tb, bw), F32),
            pltpu.VMEM((hb, tb, CHUNK), F32),
            pltpu.VMEM((hb, tb, LANES), F32),
            pltpu.VMEM((hb, tb, LANES), F32),
        ],
        compiler_params=_cparams("parallel", "parallel", "arbitrary"),
        name="gdn",
    )(proj, proj, proj, proj, c1, c2t3, conv_w, conv_w, conv_w, norm_w)


def _ssd_kernel(x_ref, bm_ref, cm_ref, z_ref, c1_ref, c2_ref, arow_ref,
                wx_ref, wb_ref, wc_ref, bx_ref, bb_ref, bc_ref, dskip_ref, nw_ref,
                o_ref, state_ref, tail_ref, x_s, bm_s, cm_s, dt_s, ax_s, acol_s,
                *, dt_col0, heads_per_group):
    tb, width = x_ref.shape
    R = heads_per_group
    P = SSM_HEAD_DIM
    g = pl.program_id(1)

    @pl.when(pl.program_id(2) == 0)
    def _():
        state_ref[...] = jnp.zeros_like(state_ref)
        tail_ref[...] = jnp.zeros_like(tail_ref)

    def conv(ref, lo, hi, w_ref, b_ref):
        x = ref[...]
        xe = jnp.concatenate([tail_ref[:, lo:hi], x], axis=0)
        w = w_ref[...]
        y = w[SHORT_CONV - 1:SHORT_CONV, :] * x
        for k in range(SHORT_CONV - 1):
            off = HALO - (SHORT_CONV - 1) + k
            y = y + w[k:k + 1, :] * xe[off:off + tb, :]
        tail_ref[:, lo:hi] = x[tb - HALO:, :]
        return _silu(y + b_ref[...])

    x_s[...] = conv(x_ref, 0, width, wx_ref, bx_ref)
    bm_s[...] = conv(bm_ref, width, width + SSM_STATE, wb_ref, bb_ref)
    cm_s[...] = conv(cm_ref, width + SSM_STATE, width + 2 * SSM_STATE, wc_ref, bc_ref)

    c1 = c1_ref[...]
    c2 = c2_ref[...]
    lane = lax.broadcasted_iota(jnp.int32, c1.shape, 1)
    grp = jnp.right_shift(lax.broadcasted_iota(jnp.int32, (tb, width), 1), SSM_HEAD_SHIFT)
    dtx = jnp.zeros((tb, width), F32)
    ax = jnp.zeros((tb, width), F32)
    for r in range(R):
        idx = dt_col0 + g * R + r
        dcol = _pick_col(c1, lane, idx)
        acol = _pick_col(c2, lane, idx)
        dtx = jnp.where(grp == r, dcol, dtx)
        ax = jnp.where(grp == r, acol, ax)
        acol_s[r] = jnp.broadcast_to(acol, (tb, LANES))
    dt_s[...] = dtx
    ax_s[...] = ax

    ri = lax.broadcasted_iota(jnp.int32, (CHUNK, CHUNK), 0)
    ci = lax.broadcasted_iota(jnp.int32, (CHUNK, CHUNK), 1)
    causal = ri >= ci
    grp_c = jnp.right_shift(lax.broadcasted_iota(jnp.int32, (CHUNK, width), 1), SSM_HEAD_SHIFT)

    def chunk(c, carry):
        r0 = pl.multiple_of(c * CHUNK, CHUNK)
        rows = pl.ds(r0, CHUNK)
        x = x_s[rows, :]
        bm = bm_s[rows, :]
        cm = cm_s[rows, :]
        a = ax_s[rows, :]
        xdt = x * dt_s[rows, :]
        cb = _dot_nt(cm, bm)
        ydiag = jnp.zeros((CHUNK, width), F32)
        for r in range(R):
            arow = arow_ref[r, pl.ds(c, 1), :]
            acol = acol_s[r, rows, :][:, :CHUNK]
            lmat = jnp.exp(jnp.where(causal, acol - arow, NEG_BIG))
            ydiag = ydiag + _dot(cb * lmat, jnp.where(grp_c == r, xdt, 0.0))
        alast = a[CHUNK - 1:CHUNK, :]
        hstate = state_ref[...]
        yoff = _dot(cm, hstate) * jnp.exp(a)
        state_ref[...] = hstate * jnp.exp(alast) + _dot_tn(bm, xdt * jnp.exp(alast - a))
        y = ydiag + yoff + x * dskip_ref[...]
        y = y * _silu(z_ref[rows, :])
        o_ref[rows, :] = _rms_rows(y, nw_ref[...]).astype(o_ref.dtype)
        return carry

    lax.fori_loop(0, tb // CHUNK, chunk, 0)


def _ssd(proj, c1, c2, c2t3, conv_w, conv_b, dskip_x, norm_w, *, batch, seq, n_groups,
         heads_per_group, z_col0, x_col0, dt_col0):
    T = proj.shape[0]
    R = heads_per_group
    width = R * SSM_HEAD_DIM
    ssm_width = n_groups * width
    ns = SSM_STATE
    tb = _pick(seq, (512, 256, 128, 64))
    nt = seq // tb
    cpb = tb // CHUNK
    assert x_col0 % width == 0 and z_col0 % width == 0 and dt_col0 % R == 0
    assert (x_col0 + ssm_width) % ns == 0 and ssm_width % ns == 0
    xb0 = x_col0 // width
    zb0 = z_col0 // width
    bb0 = (x_col0 + ssm_width) // ns
    cb0 = bb0 + n_groups
    wb0 = ssm_width // ns

    def rows(b, g, t):
        return b * nt + t

    kern = functools.partial(_ssd_kernel, dt_col0=dt_col0, heads_per_group=R)
    return pl.pallas_call(
        kern,
        grid=(batch, n_groups, nt),
        in_specs=[
            pl.BlockSpec((tb, width), lambda b, g, t: (rows(b, g, t), xb0 + g)),
            pl.BlockSpec((tb, ns), lambda b, g, t: (rows(b, g, t), bb0 + g)),
            pl.BlockSpec((tb, ns), lambda b, g, t: (rows(b, g, t), cb0 + g)),
            pl.BlockSpec((tb, width), lambda b, g, t: (rows(b, g, t), zb0 + g)),
            pl.BlockSpec((tb, LANES), lambda b, g, t: (rows(b, g, t), 0)),
            pl.BlockSpec((tb, LANES), lambda b, g, t: (rows(b, g, t), 0)),
            pl.BlockSpec((R, cpb, CHUNK), lambda b, g, t: (dt_col0 // R + g, rows(b, g, t), 0)),
            pl.BlockSpec((SHORT_CONV, width), lambda b, g, t: (0, g)),
            pl.BlockSpec((SHORT_CONV, ns), lambda b, g, t: (0, wb0 + g)),
            pl.BlockSpec((SHORT_CONV, ns), lambda b, g, t: (0, wb0 + n_groups + g)),
            pl.BlockSpec((1, width), lambda b, g, t: (0, g)),
            pl.BlockSpec((1, ns), lambda b, g, t: (0, wb0 + g)),
            pl.BlockSpec((1, ns), lambda b, g, t: (0, wb0 + n_groups + g)),
            pl.BlockSpec((1, width), lambda b, g, t: (0, g)),
            pl.BlockSpec((1, width), lambda b, g, t: (0, g)),
        ],
        out_specs=pl.BlockSpec((tb, width), lambda b, g, t: (rows(b, g, t), g)),
        out_shape=jax.ShapeDtypeStruct((T, ssm_width), BF16),
        scratch_shapes=[
            pltpu.VMEM((ns, width), F32),
            pltpu.VMEM((HALO, width + 2 * ns), F32),
            pltpu.VMEM((tb, width), F32),
            pltpu.VMEM((tb, ns), F32),
            pltpu.VMEM((tb, ns), F32),
            pltpu.VMEM((tb, width), F32),
            pltpu.VMEM((tb, width), F32),
            pltpu.VMEM((R, tb, LANES), F32),
        ],
        compiler_params=_cparams("parallel", "parallel", "arbitrary"),
        name="ssd",
    )(proj, proj, proj, proj, c1, c2, c2t3, conv_w, conv_w, conv_w, conv_b, conv_b, conv_b,
      dskip_x, norm_w)


def _outproj_kernel(x_ref, yg_ref, ys_ref, wg_ref, ws_ref, o_ref):
    acc = jnp.dot(yg_ref[...], wg_ref[...], preferred_element_type=F32)
    acc = acc + jnp.dot(ys_ref[...], ws_ref[...], preferred_element_type=F32)
    o_ref[...] = x_ref[...] + acc


def _out_proj(x2, y_gdn, y_ssm, w_g, w_s):
    T, D = x2.shape
    kg, ks = y_gdn.shape[1], y_ssm.shape[1]
    tm = _pick(T, (512, 256, 128))
    tn = _pick(D, (1024, 512, 256, 128))
    return pl.pallas_call(
        _outproj_kernel,
        grid=(T // tm, D // tn),
        in_specs=[
            pl.BlockSpec((tm, tn), lambda i, j: (i, j)),
            pl.BlockSpec((tm, kg), lambda i, j: (i, 0)),
            pl.BlockSpec((tm, ks), lambda i, j: (i, 0)),
            pl.BlockSpec((kg, tn), lambda i, j: (0, j)),
            pl.BlockSpec((ks, tn), lambda i, j: (0, j)),
        ],
        out_specs=pl.BlockSpec((tm, tn), lambda i, j: (i, j)),
        out_shape=jax.ShapeDtypeStruct((T, D), F32),
        compiler_params=_cparams("parallel", "arbitrary"),
        name="out_proj",
    )(x2, y_gdn, y_ssm, w_g, w_s)


def _ffnup_kernel(h_ref, halo_ref, nw_ref, wg_ref, wu_ref, cw_ref, cb_ref, o_ref, hn_ref,
                  *, blocks_per_seq):
    tm = h_ref.shape[0]
    i = pl.program_id(0)

    @pl.when(pl.program_id(1) == 0)
    def _():
        nw = nw_ref[...]
        keep = jnp.where(i % blocks_per_seq == 0, 0.0, 1.0)
        hn_ref[0:HALO_BF16, :] = (_rms_rows(halo_ref[...], nw) * keep).astype(BF16)
        hn_ref[HALO_BF16:, :] = _rms_rows(h_ref[...], nw).astype(BF16)

    hn = hn_ref[...]
    ge = jnp.dot(hn, wg_ref[...], preferred_element_type=F32)
    up = jnp.dot(hn_ref[HALO_BF16:, :], wu_ref[...], preferred_element_type=F32)
    cw = cw_ref[...]
    g = cb_ref[...] + cw[FFN_CONV - 1:FFN_CONV, :] * ge[HALO_BF16:, :]
    for k in range(FFN_CONV - 1):
        off = HALO_BF16 - (FFN_CONV - 1) + k
        g = g + cw[k:k + 1, :] * ge[off:off + tm, :]
    o_ref[...] = (_silu(g) * up).astype(o_ref.dtype)


def _ffn_up(h2, norm_w, w_gate, w_up, conv_w, conv_b, *, seq):
    T, D = h2.shape
    F = w_gate.shape[1]
    tm = _pick(seq, (512, 256, 128))
    tn = _pick(F, (512, 256, 128))
    kern = functools.partial(_ffnup_kernel, blocks_per_seq=seq // tm)
    hb = tm // HALO_BF16
    return pl.pallas_call(
        kern,
        grid=(T // tm, F // tn),
        in_specs=[
            pl.BlockSpec((tm, D), lambda i, j: (i, 0)),
            pl.BlockSpec((HALO_BF16, D), lambda i, j: (jnp.maximum(i * hb - 1, 0), 0)),
            pl.BlockSpec((1, D), lambda i, j: (0, 0)),
            pl.BlockSpec((D, tn), lambda i, j: (0, j)),
            pl.BlockSpec((D, tn), lambda i, j: (0, j)),
            pl.BlockSpec((FFN_CONV, tn), lambda i, j: (0, j)),
            pl.BlockSpec((1, tn), lambda i, j: (0, j)),
        ],
        out_specs=pl.BlockSpec((tm, tn), lambda i, j: (i, j)),
        out_shape=jax.ShapeDtypeStruct((T, F), BF16),
        scratch_shapes=[pltpu.VMEM((HALO_BF16 + tm, D), BF16)],
        compiler_params=_cparams("parallel", "arbitrary"),
        name="ffn_up",
    )(h2, h2, norm_w, w_gate, w_up, conv_w, conv_b)


def _ffndown_kernel(a_ref, w_ref, h_ref, nw_ref, o_ref, *, final_norm):
    k = pl.program_id(1)

    @pl.when(k == 0)
    def _():
        o_ref[...] = h_ref[...]

    o_ref[...] += jnp.dot(a_ref[...], w_ref[...], preferred_element_type=F32)

    if final_norm:
        @pl.when(k == pl.num_programs(1) - 1)
        def _():
            o_ref[...] = _rms_rows(o_ref[...], nw_ref[...])


def _ffn_down(act, w_down, h2, norm_w, *, final_norm):
    T, F = act.shape
    D = w_down.shape[1]
    tm = _pick(T, (512, 256, 128))
    tk = _pick(F, (512, 256, 128))
    return pl.pallas_call(
        functools.partial(_ffndown_kernel, final_norm=final_norm),
        grid=(T // tm, F // tk),
        in_specs=[
            pl.BlockSpec((tm, tk), lambda i, k: (i, k)),
            pl.BlockSpec((tk, D), lambda i, k: (k, 0)),
            pl.BlockSpec((tm, D), lambda i, k: (i, 0)),
            pl.BlockSpec((1, D), lambda i, k: (0, 0)),
        ],
        out_specs=pl.BlockSpec((tm, D), lambda i, k: (i, 0)),
        out_shape=jax.ShapeDtypeStruct((T, D), F32),
        compiler_params=_cparams("parallel", "arbitrary"),
        name="ffn_down",
    )(act, w_down, h2, norm_w)


def _pad_to(a, axis, size):
    pad = size - a.shape[axis]
    if pad == 0:
        return a
    widths = [(0, 0)] * a.ndim
    widths[axis] = (0, pad)
    return jnp.pad(a, widths)


def kernel(x, norm1_w, w_in, gdn_conv_w, gdn_A_log, gdn_dt_bias, gdn_norm_w, ssm_conv_w, ssm_conv_b,
           ssm_A_log, ssm_dt_bias, ssm_D, ssm_norm_w, w_out, norm2_w, ffn_w_gate, ffn_w_up,
           ffn_conv_w, ffn_conv_b, ffn_w_down, norm_f_w):
    B, L, D = x.shape
    depth = w_in.shape[0]
    gh = gdn_A_log.shape[1]
    sh = ssm_A_log.shape[1]
    gw = gh * GDN_HEAD_DIM
    sw = sh * SSM_HEAD_DIM
    gn = (ssm_conv_w.shape[-1] - sw) // 2
    n_groups = gn // SSM_STATE
    hpg = sh // n_groups
    assert 2 * gh + sh <= LANES and L % CHUNK == 0
    T = B * L
    f_dim = ffn_w_gate.shape[-1]
    f_pad = -(-f_dim // 1024) * 1024 if f_dim >= 1024 else f_dim

    o_gate = 3 * gw
    o_b = o_gate + gw
    o_a = o_b + gh
    o_z = o_a + gh
    o_xbc = o_z + sw
    o_dt = o_xbc + sw + 2 * gn

    h2 = x.reshape(T, D)
    for i in range(depth):
        wi = w_in[i]
        w_main = jnp.concatenate([wi[:, :o_b], wi[:, o_z:o_dt]], axis=1).astype(BF16)
        w_small = _pad_to(jnp.concatenate([wi[:, o_b:o_z], wi[:, o_dt:]], axis=1), 1, LANES).astype(BF16)
        z_col0 = o_b
        x_col0 = o_b + sw

        proj, small = _in_proj(h2, norm1_w[i][None, :], w_main, w_small)

        bias_row = _pad_to(jnp.concatenate([jnp.zeros((gh,), F32), gdn_dt_bias[i], ssm_dt_bias[i]])[None, :], 1, LANES)
        alog_row = _pad_to(jnp.concatenate([jnp.zeros((gh,), F32), gdn_A_log[i], ssm_A_log[i]])[None, :], 1, LANES)
        c1, c2, c2t = _prep(small, bias_row, alog_row, gh)
        c2t3 = c2t.reshape(LANES, T // CHUNK, CHUNK)

        y_gdn = _gdn(proj, c1, c2t3, gdn_conv_w[i], gdn_norm_w[i][None, :],
                     batch=B, seq=L, n_heads=gh, gate_col0=o_gate // GDN_HEAD_DIM)
        y_ssm = _ssd(proj, c1, c2, c2t3, ssm_conv_w[i], ssm_conv_b[i][None, :],
                     jnp.repeat(ssm_D[i], SSM_HEAD_DIM)[None, :], ssm_norm_w[i][None, :],
                     batch=B, seq=L, n_groups=n_groups, heads_per_group=hpg,
                     z_col0=z_col0, x_col0=x_col0, dt_col0=2 * gh)

        wo = w_out[i].astype(BF16)
        h2 = _out_proj(h2, y_gdn, y_ssm, wo[:gw], wo[gw:])

        act = _ffn_up(h2, norm2_w[i][None, :],
                      _pad_to(ffn_w_gate[i], 1, f_pad).astype(BF16),
                      _pad_to(ffn_w_up[i], 1, f_pad).astype(BF16),
                      _pad_to(ffn_conv_w[i], 1, f_pad), _pad_to(ffn_conv_b[i][None, :], 1, f_pad), seq=L)
        h2 = _ffn_down(act, _pad_to(ffn_w_down[i], 0, f_pad).astype(BF16), h2, norm_f_w[None, :],
                       final_norm=(i == depth - 1))
    return h2.reshape(B, L, D)
```

```python
import functools
import math

import jax
import jax.numpy as jnp
from jax import lax
from jax.experimental import pallas as pl
from jax.experimental.pallas import tpu as pltpu

F32 = jnp.float32
BF16 = jnp.bfloat16

EPS = 1e-6
CHUNK = 64
CHUNK_SHIFT = 6
GDN_HEAD_DIM = 128
GDN_HEADS_PER_STEP = 8
GDN_CHUNKS_PER_ITER = 2
SSD_CHUNKS_PER_ITER = 4
INV_BASE = 8
SSM_HEAD_DIM = 64
SSM_HEAD_SHIFT = 6
SSM_STATE = 128
SHORT_CONV = 4
FFN_CONV = 3
LANES = 128
SUBLANES = 8
HALO_BF16 = 2 * SUBLANES
NEG_BIG = -1e30
VMEM_LIMIT = 60 * 1024 * 1024


def _pick(n, candidates):
    for c in candidates:
        if n % c == 0:
            return c
    raise ValueError(f"no tile in {candidates} divides {n}")


def _cparams(*sem):
    return pltpu.CompilerParams(dimension_semantics=sem, vmem_limit_bytes=VMEM_LIMIT)


def _silu(x):
    return x * (1.0 / (1.0 + jnp.exp(-x)))


def _dot(a, b):
    return jnp.dot(a.astype(BF16), b.astype(BF16), preferred_element_type=F32)


def _dot_nt(a, b):
    return lax.dot_general(a.astype(BF16), b.astype(BF16), (((1,), (1,)), ((), ())),
                           preferred_element_type=F32)


def _dot_tn(a, b):
    return lax.dot_general(a.astype(BF16), b.astype(BF16), (((0,), (0,)), ((), ())),
                           preferred_element_type=F32)


def _dot_f32(a, b):
    return jnp.dot(a, b, preferred_element_type=F32, precision=lax.Precision.HIGHEST)


def _rms_rows(x, w):
    return x * lax.rsqrt(jnp.mean(x * x, axis=-1, keepdims=True) + EPS) * w


def _each(f, *lists):
    return [f(*xs) for xs in zip(*lists)]


def _fill_normed(hn_ref, halo_ref, x_ref, nw_ref, first_of_seq):
    nw = nw_ref[...]
    keep = jnp.where(first_of_seq, 0.0, 1.0)
    hn_ref[0:HALO_BF16, :] = (_rms_rows(halo_ref[...], nw) * keep).astype(BF16)
    hn_ref[HALO_BF16:, :] = _rms_rows(x_ref[...], nw).astype(BF16)


def _causal_taps(ye, cw, cb, width, tm):
    y = cb + cw[width - 1:width, :] * ye[HALO_BF16:, :]
    for k in range(width - 1):
        off = HALO_BF16 - (width - 1) + k
        y = y + cw[k:k + 1, :] * ye[off:off + tm, :]
    return y


def _inproj_kernel(x_ref, halo_ref, nw_ref, w_ref, ws_ref, cw_ref, cb_ref, o_ref, os_ref, hn_ref,
                   *, blocks_per_seq, n_q_tiles, n_qkv_tiles, first_xbc_tile):
    tm, tn = o_ref.shape
    i = pl.program_id(0)
    j = pl.program_id(1)

    @pl.when(j == 0)
    def _():
        _fill_normed(hn_ref, halo_ref, x_ref, nw_ref, i % blocks_per_seq == 0)
        os_ref[...] = jnp.dot(hn_ref[HALO_BF16:, :], ws_ref[...], preferred_element_type=F32)

    has_conv = jnp.logical_or(j < n_qkv_tiles, j >= first_xbc_tile)

    @pl.when(jnp.logical_not(has_conv))
    def _():
        o_ref[...] = jnp.dot(hn_ref[HALO_BF16:, :], w_ref[...], preferred_element_type=F32)

    @pl.when(has_conv)
    def _():
        cw = cw_ref[...]
        cb = cb_ref[...]
        is_qk = j < 2 * n_q_tiles
        q_scale = jnp.where(j < n_q_tiles, GDN_HEAD_DIM ** -0.5, 1.0)
        ye = jnp.dot(hn_ref[...], w_ref[...], preferred_element_type=F32)
        for g in range(0, tn, LANES):
            cols = slice(g, g + LANES)
            y = _silu(_causal_taps(ye[:, cols], cw[:, cols], cb[:, cols], SHORT_CONV, tm))
            l2 = lax.rsqrt(jnp.sum(y * y, axis=-1, keepdims=True) + EPS) * q_scale
            o_ref[:, cols] = y * jnp.where(is_qk, l2, 1.0)


def _in_proj(x2, norm_w, w_main, w_small, conv_w, conv_b, *, seq, q_width, x_col0):
    T, D = x2.shape
    N = w_main.shape[1]
    tm = _pick(seq, (512, 256, 128))
    tn = _pick(math.gcd(math.gcd(q_width, x_col0), N), (1024, 512, 256, 128))
    hb = tm // HALO_BF16
    kern = functools.partial(_inproj_kernel, blocks_per_seq=seq // tm, n_q_tiles=q_width // tn,
                             n_qkv_tiles=3 * q_width // tn, first_xbc_tile=x_col0 // tn)
    return pl.pallas_call(
        kern,
        grid=(T // tm, N // tn),
        in_specs=[
            pl.BlockSpec((tm, D), lambda i, j: (i, 0)),
            pl.BlockSpec((HALO_BF16, D), lambda i, j: (jnp.maximum(i * hb - 1, 0), 0)),
            pl.BlockSpec((1, D), lambda i, j: (0, 0)),
            pl.BlockSpec((D, tn), lambda i, j: (0, j)),
            pl.BlockSpec((D, LANES), lambda i, j: (0, 0)),
            pl.BlockSpec((SHORT_CONV, tn), lambda i, j: (0, j)),
            pl.BlockSpec((1, tn), lambda i, j: (0, j)),
        ],
        out_specs=[
            pl.BlockSpec((tm, tn), lambda i, j: (i, j)),
            pl.BlockSpec((tm, LANES), lambda i, j: (i, 0)),
        ],
        out_shape=[jax.ShapeDtypeStruct((T, N), F32), jax.ShapeDtypeStruct((T, LANES), F32)],
        scratch_shapes=[pltpu.VMEM((HALO_BF16 + tm, D), BF16)],
        compiler_params=_cparams("parallel", "arbitrary"),
        name="in_proj",
    )(x2, x2, norm_w, w_main, w_small, conv_w, conv_b)


def _prep_kernel(s_ref, bias_ref, alog_ref, c1_ref, c2_ref, c2t_ref, *, n_beta, n_gdn):
    tb = s_ref.shape[0]
    s = s_ref[...]
    lane = lax.broadcasted_iota(jnp.int32, s.shape, 1)
    xb = s + bias_ref[...]
    sp = jnp.maximum(xb, 0.0) + jnp.log(1.0 + jnp.exp(-jnp.abs(xb)))
    sig = 1.0 / (1.0 + jnp.exp(-s))
    a_neg = -jnp.exp(alog_ref[...])
    step = jnp.where(lane >= n_beta, sp * a_neg, 0.0)
    row = lax.broadcasted_iota(jnp.int32, (tb, tb), 0)
    col = lax.broadcasted_iota(jnp.int32, (tb, tb), 1)
    same_chunk = jnp.right_shift(row, CHUNK_SHIFT) == jnp.right_shift(col, CHUNK_SHIFT)
    tri = jnp.where(same_chunk, jnp.where(row >= col, 1.0, 0.0), 0.0).astype(F32)
    cs = _dot_f32(tri, step)
    c1_ref[...] = jnp.where(lane < n_beta, sig, jnp.where(lane < n_gdn, cs, sp))
    c2_ref[...] = cs
    c2t_ref[...] = cs.T


def _prep(small, bias_row, alog_row, n_gdn_heads):
    T = small.shape[0]
    tb = _pick(T, (256, 128, 64))
    kern = functools.partial(_prep_kernel, n_beta=n_gdn_heads, n_gdn=2 * n_gdn_heads)
    return pl.pallas_call(
        kern,
        grid=(T // tb,),
        in_specs=[
            pl.BlockSpec((tb, LANES), lambda i: (i, 0)),
            pl.BlockSpec((1, LANES), lambda i: (0, 0)),
            pl.BlockSpec((1, LANES), lambda i: (0, 0)),
        ],
        out_specs=[
            pl.BlockSpec((tb, LANES), lambda i: (i, 0)),
            pl.BlockSpec((tb, LANES), lambda i: (i, 0)),
            pl.BlockSpec((LANES, tb), lambda i: (0, i)),
        ],
        out_shape=[
            jax.ShapeDtypeStruct((T, LANES), F32),
            jax.ShapeDtypeStruct((T, LANES), F32),
            jax.ShapeDtypeStruct((LANES, T), F32),
        ],
        compiler_params=_cparams("parallel"),
        name="prep",
    )(small, bias_row, alog_row)


def _pick_col(cc, lane, idx):
    return jnp.sum(jnp.where(lane == idx, cc, 0.0), axis=-1, keepdims=True)


def _inv_unit_lower_many(mats):
    n = mats[0].shape[0]
    ri = lax.broadcasted_iota(jnp.int32, (n, n), 0)
    ci = lax.broadcasted_iota(jnp.int32, (n, n), 1)
    eye = jnp.where(ri == ci, 1.0, 0.0).astype(F32)
    blk = ri ^ ci
    ps = _each(lambda a: jnp.where(blk < INV_BASE, -a, 0.0), mats)
    ts = _each(lambda p: eye + p, ps)
    m = 2
    while m < INV_BASE:
        ps = _each(lambda p: _dot(p, p), ps)
        ts = _each(lambda t, p: t + _dot(t, p), ts, ps)
        m *= 2
    m = INV_BASE
    while m < n:
        offs = _each(lambda a: jnp.where(blk >= m, jnp.where(blk < 2 * m, a, 0.0), 0.0), mats)
        xs = _each(_dot, offs, ts)
        ts = _each(lambda t, x: t - _dot(t, x), ts, xs)
        m *= 2
    return ts


def _gdn_kernel(q_ref, k_ref, v_ref, gate_ref, c1_ref, grow_ref, nw_ref,
                o_ref, state_ref, u_s, w_s, qd_s, kd_s, attn_s, beta_s, gc_s,
                *, n_heads, hb, chunks_per_iter):
    tb = q_ref.shape[0]
    hd = GDN_HEAD_DIM
    h0 = pl.program_id(1) * hb
    n_chunks = tb // CHUNK
    heads = list(range(hb))
    head_cols = [slice(hh * hd, (hh + 1) * hd) for hh in heads]

    @pl.when(pl.program_id(2) == 0)
    def _():
        state_ref[...] = jnp.zeros_like(state_ref)

    cc = c1_ref[...]
    lane = lax.broadcasted_iota(jnp.int32, cc.shape, 1)
    for hh in heads:
        beta_s[hh] = jnp.broadcast_to(_pick_col(cc, lane, h0 + hh), cc.shape)
        gc_s[hh] = jnp.broadcast_to(_pick_col(cc, lane, n_heads + h0 + hh), cc.shape)

    ri = lax.broadcasted_iota(jnp.int32, (CHUNK, CHUNK), 0)
    ci = lax.broadcasted_iota(jnp.int32, (CHUNK, CHUNK), 1)
    causal = ri >= ci
    strict = ri > ci

    def intra_load(c, hh):
        rows = pl.ds(pl.multiple_of(c * CHUNK, CHUNK), CHUNK)
        cols = head_cols[hh]
        return (q_ref[rows, cols], k_ref[rows, cols], v_ref[rows, cols], beta_s[hh, rows, :],
                gc_s[hh, rows, :], grow_ref[hh, pl.ds(c, 1), :])

    def intra_compute(q, k, v, beta, gcol, grow):
        gamma = _each(lambda gc, gr: jnp.exp(jnp.where(causal, gc[:, :CHUNK] - gr, NEG_BIG)), gcol, grow)
        kb = _each(jnp.multiply, k, beta)
        a = _each(lambda kb_, k_, gm: jnp.where(strict, _dot_nt(kb_, k_) * gm, 0.0), kb, k, gamma)
        attn = _each(lambda q_, k_, gm: _dot_nt(q_, k_) * gm, q, k, gamma)
        t_inv = _inv_unit_lower_many(a)
        eg = _each(jnp.exp, gcol)
        u = _each(lambda t, v_, b: _dot(t, v_ * b), t_inv, v, beta)
        w = _each(lambda t, kb_, e: _dot(t, kb_ * e), t_inv, kb, eg)
        q_dec = _each(jnp.multiply, q, eg)
        k_dec = _each(lambda k_, gc: k_ * jnp.exp(gc[CHUNK - 1:CHUNK, :] - gc), k, gcol)
        return list(zip(u, w, attn, q_dec, k_dec))

    def intra_store(c, hh, u, w, attn, q_dec, k_dec):
        rows = pl.ds(pl.multiple_of(c * CHUNK, CHUNK), CHUNK)
        cols = head_cols[hh]
        u_s[rows, cols] = u
        w_s[rows, cols] = w.astype(w_s.dtype)
        attn_s[hh, rows, :] = attn.astype(attn_s.dtype)
        qd_s[rows, cols] = q_dec.astype(qd_s.dtype)
        kd_s[rows, cols] = k_dec.astype(kd_s.dtype)

    def pass1(it, carry):
        work = [(it * chunks_per_iter + cc_, hh) for cc_ in range(chunks_per_iter) for hh in heads]
        loaded = [intra_load(c, hh) for c, hh in work]
        results = intra_compute(*[list(col) for col in zip(*loaded)])
        for (c, hh), res in zip(work, results):
            intra_store(c, hh, *res)
        return carry

    def pass2(c, carry):
        r0 = pl.multiple_of(c * CHUNK, CHUNK)
        rows = pl.ds(r0, CHUNK)
        s = [state_ref[hh] for hh in heads]
        u = [u_s[rows, cols] for cols in head_cols]
        w = [w_s[rows, cols] for cols in head_cols]
        qd = [qd_s[rows, cols] for cols in head_cols]
        kd = [kd_s[rows, cols] for cols in head_cols]
        attn = [attn_s[hh, rows, :] for hh in heads]
        dec = [jnp.exp(gc_s[hh, pl.ds(r0 + CHUNK - 1, 1), :]) for hh in heads]
        gate = [gate_ref[rows, cols] for cols in head_cols]
        ws = _each(_dot, w, s)
        qs = _each(_dot, qd, s)
        v_new = _each(jnp.subtract, u, ws)
        kv = _each(_dot_tn, kd, v_new)
        av = _each(_dot, attn, v_new)
        for hh in heads:
            state_ref[hh] = s[hh] * dec[hh] + kv[hh]
        nw = nw_ref[...]
        for hh in heads:
            o = _rms_rows(qs[hh] + av[hh], nw) * _silu(gate[hh])
            o_ref[rows, head_cols[hh]] = o.astype(o_ref.dtype)
        return carry

    lax.fori_loop(0, n_chunks // chunks_per_iter, pass1, 0)
    lax.fori_loop(0, n_chunks, pass2, 0)


def _gdn(proj, c1, c2t3, norm_w, *, batch, seq, n_heads, gate_col0):
    T = proj.shape[0]
    tb = _pick(seq, (512, 256, 128, 64))
    nt = seq // tb
    hd = GDN_HEAD_DIM
    cpb = tb // CHUNK
    hb = _pick(n_heads, (GDN_HEADS_PER_STEP, 4, 2, 1))
    cpi = _pick(cpb, (GDN_CHUNKS_PER_ITER, 2, 1))
    bw = hb * hd
    ng = n_heads // hb
    assert gate_col0 % hb == 0

    def rows(b, h, t):
        return b * nt + t

    kern = functools.partial(_gdn_kernel, n_heads=n_heads, hb=hb, chunks_per_iter=cpi)
    return pl.pallas_call(
        kern,
        grid=(batch, ng, nt),
        in_specs=[
            pl.BlockSpec((tb, bw), lambda b, h, t: (rows(b, h, t), h)),
            pl.BlockSpec((tb, bw), lambda b, h, t: (rows(b, h, t), ng + h)),
            pl.BlockSpec((tb, bw), lambda b, h, t: (rows(b, h, t), 2 * ng + h)),
            pl.BlockSpec((tb, bw), lambda b, h, t: (rows(b, h, t), gate_col0 // hb + h)),
            pl.BlockSpec((tb, LANES), lambda b, h, t: (rows(b, h, t), 0)),
            pl.BlockSpec((hb, cpb, CHUNK), lambda b, h, t: (ng + h, rows(b, h, t), 0)),
            pl.BlockSpec((1, hd), lambda b, h, t: (0, 0)),
        ],
        out_specs=pl.BlockSpec((tb, bw), lambda b, h, t: (rows(b, h, t), h)),
        out_shape=jax.ShapeDtypeStruct((T, n_heads * hd), BF16),
        scratch_shapes=[
            pltpu.VMEM((hb, hd, hd), F32),
            pltpu.VMEM((tb, bw), F32),
            pltpu.VMEM((tb, bw), BF16),
            pltpu.VMEM((tb, bw), BF16),
            pltpu.VMEM((tb, bw), BF16),
            pltpu.VMEM((hb, tb, CHUNK), BF16),
            pltpu.VMEM((hb, tb, LANES), F32),
            pltpu.VMEM((hb, tb, LANES), F32),
        ],
        compiler_params=_cparams("parallel", "parallel", "arbitrary"),
        name="gdn",
    )(proj, proj, proj, proj, c1, c2t3, norm_w)


def _ssd_kernel(x_ref, bm_ref, cm_ref, z_ref, c1_ref, c2_ref, arow_ref, dskip_ref, nw_ref,
                o_ref, state_ref, dt_s, ax_s, *, dt_col0, heads_per_group, chunks_per_iter):
    tb, width = x_ref.shape
    R = heads_per_group
    g = pl.program_id(1)

    @pl.when(pl.program_id(2) == 0)
    def _():
        state_ref[...] = jnp.zeros_like(state_ref)

    c1 = c1_ref[...]
    c2 = c2_ref[...]
    lane = lax.broadcasted_iota(jnp.int32, c1.shape, 1)
    grp = jnp.right_shift(lax.broadcasted_iota(jnp.int32, (tb, width), 1), SSM_HEAD_SHIFT)
    dtx = jnp.zeros((tb, width), F32)
    ax = jnp.zeros((tb, width), F32)
    for r in range(R):
        idx = dt_col0 + g * R + r
        dtx = jnp.where(grp == r, _pick_col(c1, lane, idx), dtx)
        ax = jnp.where(grp == r, _pick_col(c2, lane, idx), ax)
    dt_s[...] = dtx
    ax_s[...] = ax

    ri = lax.broadcasted_iota(jnp.int32, (CHUNK, width), 0)
    ci = lax.broadcasted_iota(jnp.int32, (CHUNK, width), 1)
    causal = ri >= (ci & (CHUNK - 1))
    grp_c = jnp.right_shift(ci, SSM_HEAD_SHIFT)
    dskip = dskip_ref[...]
    nw = nw_ref[...]

    def body(it, carry):
        cs = [it * chunks_per_iter + k for k in range(chunks_per_iter)]
        rows = [pl.ds(pl.multiple_of(c * CHUNK, CHUNK), CHUNK) for c in cs]
        x = [x_ref[r, :] for r in rows]
        bm = [bm_ref[r, :] for r in rows]
        cm = [cm_ref[r, :] for r in rows]
        a = [ax_s[r, :] for r in rows]
        dt = [dt_s[r, :] for r in rows]
        arow = [arow_ref[0, pl.ds(c, 1), :] for c in cs]
        xdt = _each(jnp.multiply, x, dt)
        alast = _each(lambda a_: a_[CHUNK - 1:CHUNK, :], a)
        cb = _each(lambda cm_, bm_: _dot_nt(cm_, jnp.concatenate([bm_] * R, axis=0)), cm, bm)
        states = _each(lambda bm_, xd, al, a_: _dot_tn(bm_, xd * jnp.exp(al - a_)), bm, xdt, alast, a)
        h = state_ref[...]
        h_before = []
        for k in range(chunks_per_iter):
            h_before.append(h)
            h = h * jnp.exp(alast[k]) + states[k]
        state_ref[...] = h
        lmat = _each(lambda a_, ar: jnp.exp(jnp.where(causal, a_ - ar, NEG_BIG)), a, arow)
        xbd = _each(lambda xd: jnp.concatenate([jnp.where(grp_c == r, xd, 0.0) for r in range(R)], axis=0), xdt)
        ydiag = _each(lambda cb_, lm, xb: _dot(cb_ * lm, xb), cb, lmat, xbd)
        yoff = _each(lambda cm_, hp, a_: _dot(cm_, hp) * jnp.exp(a_), cm, h_before, a)
        for k in range(chunks_per_iter):
            y = ydiag[k] + yoff[k] + x[k] * dskip
            y = y * _silu(z_ref[rows[k], :])
            o_ref[rows[k], :] = _rms_rows(y, nw).astype(o_ref.dtype)
        return carry

    lax.fori_loop(0, tb // CHUNK // chunks_per_iter, body, 0)


def _ssd(proj, c1, c2, arow_all, dskip_x, norm_w, *, batch, seq, n_groups, heads_per_group,
         z_col0, x_col0, dt_col0):
    T = proj.shape[0]
    R = heads_per_group
    width = R * SSM_HEAD_DIM
    ssm_width = n_groups * width
    ns = SSM_STATE
    tb = _pick(seq, (512, 256, 128, 64))
    nt = seq // tb
    cpb = tb // CHUNK
    cpi = _pick(cpb, (SSD_CHUNKS_PER_ITER, 2, 1))
    assert CHUNK == SSM_HEAD_DIM
    assert x_col0 % width == 0 and z_col0 % width == 0
    assert (x_col0 + ssm_width) % ns == 0
    xb0 = x_col0 // width
    zb0 = z_col0 // width
    bb0 = (x_col0 + ssm_width) // ns
    cb0 = bb0 + n_groups

    def rows(b, g, t):
        return b * nt + t

    kern = functools.partial(_ssd_kernel, dt_col0=dt_col0, heads_per_group=R, chunks_per_iter=cpi)
    return pl.pallas_call(
        kern,
        grid=(batch, n_groups, nt),
        in_specs=[
            pl.BlockSpec((tb, width), lambda b, g, t: (rows(b, g, t), xb0 + g)),
            pl.BlockSpec((tb, ns), lambda b, g, t: (rows(b, g, t), bb0 + g)),
            pl.BlockSpec((tb, ns), lambda b, g, t: (rows(b, g, t), cb0 + g)),
            pl.BlockSpec((tb, width), lambda b, g, t: (rows(b, g, t), zb0 + g)),
            pl.BlockSpec((tb, LANES), lambda b, g, t: (rows(b, g, t), 0)),
            pl.BlockSpec((tb, LANES), lambda b, g, t: (rows(b, g, t), 0)),
            pl.BlockSpec((1, cpb, width), lambda b, g, t: (g, rows(b, g, t), 0)),
            pl.BlockSpec((1, width), lambda b, g, t: (0, g)),
            pl.BlockSpec((1, width), lambda b, g, t: (0, g)),
        ],
        out_specs=pl.BlockSpec((tb, width), lambda b, g, t: (rows(b, g, t), g)),
        out_shape=jax.ShapeDtypeStruct((T, ssm_width), BF16),
        scratch_shapes=[
            pltpu.VMEM((ns, width), F32),
            pltpu.VMEM((tb, width), F32),
            pltpu.VMEM((tb, width), F32),
        ],
        compiler_params=_cparams("parallel", "parallel", "arbitrary"),
        name="ssd",
    )(proj, proj, proj, proj, c1, c2, arow_all, dskip_x, norm_w)


def _outproj_kernel(x_ref, yg_ref, ys_ref, wg_ref, ws_ref, o_ref):
    acc = jnp.dot(yg_ref[...], wg_ref[...], preferred_element_type=F32)
    acc = acc + jnp.dot(ys_ref[...], ws_ref[...], preferred_element_type=F32)
    o_ref[...] = x_ref[...] + acc


def _out_proj(x2, y_gdn, y_ssm, w_out):
    T, D = x2.shape
    kg, ks = y_gdn.shape[1], y_ssm.shape[1]
    assert kg % ks == 0
    tm = _pick(T, (512, 256, 128))
    tn = _pick(D, (1024, 512, 256, 128))
    return pl.pallas_call(
        _outproj_kernel,
        grid=(T // tm, D // tn),
        in_specs=[
            pl.BlockSpec((tm, tn), lambda i, j: (i, j)),
            pl.BlockSpec((tm, kg), lambda i, j: (i, 0)),
            pl.BlockSpec((tm, ks), lambda i, j: (i, 0)),
            pl.BlockSpec((kg, tn), lambda i, j: (0, j)),
            pl.BlockSpec((ks, tn), lambda i, j: (kg // ks, j)),
        ],
        out_specs=pl.BlockSpec((tm, tn), lambda i, j: (i, j)),
        out_shape=jax.ShapeDtypeStruct((T, D), F32),
        compiler_params=_cparams("parallel", "arbitrary"),
        name="out_proj",
    )(x2, y_gdn, y_ssm, w_out, w_out)


def _ffnup_kernel(h_ref, halo_ref, nw_ref, wg_ref, wu_ref, cw_ref, cb_ref, o_ref, hn_ref,
                  *, blocks_per_seq):
    tm = h_ref.shape[0]

    @pl.when(pl.program_id(1) == 0)
    def _():
        _fill_normed(hn_ref, halo_ref, h_ref, nw_ref, pl.program_id(0) % blocks_per_seq == 0)

    ge = jnp.dot(hn_ref[...], wg_ref[...], preferred_element_type=F32)
    up = jnp.dot(hn_ref[HALO_BF16:, :], wu_ref[...], preferred_element_type=F32)
    g = _causal_taps(ge, cw_ref[...], cb_ref[...], FFN_CONV, tm)
    o_ref[...] = (_silu(g) * up).astype(o_ref.dtype)


def _ffn_up(h2, norm_w, w_gate, w_up, conv_w, conv_b, *, seq):
    T, D = h2.shape
    F = w_gate.shape[1]
    tm = _pick(seq, (512, 256, 128))
    tn = _pick(F, (512, 256, 128))
    kern = functools.partial(_ffnup_kernel, blocks_per_seq=seq // tm)
    hb = tm // HALO_BF16
    return pl.pallas_call(
        kern,
        grid=(T // tm, F // tn),
        in_specs=[
            pl.BlockSpec((tm, D), lambda i, j: (i, 0)),
            pl.BlockSpec((HALO_BF16, D), lambda i, j: (jnp.maximum(i * hb - 1, 0), 0)),
            pl.BlockSpec((1, D), lambda i, j: (0, 0)),
            pl.BlockSpec((D, tn), lambda i, j: (0, j)),
            pl.BlockSpec((D, tn), lambda i, j: (0, j)),
            pl.BlockSpec((FFN_CONV, tn), lambda i, j: (0, j)),
            pl.BlockSpec((1, tn), lambda i, j: (0, j)),
        ],
        out_specs=pl.BlockSpec((tm, tn), lambda i, j: (i, j)),
        out_shape=jax.ShapeDtypeStruct((T, F), BF16),
        scratch_shapes=[pltpu.VMEM((HALO_BF16 + tm, D), BF16)],
        compiler_params=_cparams("parallel", "arbitrary"),
        name="ffn_up",
    )(h2, h2, norm_w, w_gate, w_up, conv_w, conv_b)


def _ffndown_kernel(a_ref, w_ref, h_ref, nw_ref, o_ref, *, final_norm):
    k = pl.program_id(1)

    @pl.when(k == 0)
    def _():
        o_ref[...] = h_ref[...]

    o_ref[...] += jnp.dot(a_ref[...], w_ref[...], preferred_element_type=F32)

    if final_norm:
        @pl.when(k == pl.num_programs(1) - 1)
        def _():
            o_ref[...] = _rms_rows(o_ref[...], nw_ref[...])


def _ffn_down(act, w_down, h2, norm_w, *, final_norm):
    T, F = act.shape
    D = w_down.shape[1]
    tm = _pick(T, (512, 256, 128))
    tk = _pick(F, (1024, 512, 256, 128))
    return pl.pallas_call(
        functools.partial(_ffndown_kernel, final_norm=final_norm),
        grid=(T // tm, F // tk),
        in_specs=[
            pl.BlockSpec((tm, tk), lambda i, k: (i, k)),
            pl.BlockSpec((tk, D), lambda i, k: (k, 0)),
            pl.BlockSpec((tm, D), lambda i, k: (i, 0), pipeline_mode=pl.Buffered(1)),
            pl.BlockSpec((1, D), lambda i, k: (0, 0)),
        ],
        out_specs=pl.BlockSpec((tm, D), lambda i, k: (i, 0)),
        out_shape=jax.ShapeDtypeStruct((T, D), F32),
        compiler_params=_cparams("parallel", "arbitrary"),
        name="ffn_down",
    )(act, w_down, h2, norm_w)


def _pad_to(a, axis, size):
    pad = size - a.shape[axis]
    if pad == 0:
        return a
    widths = [(0, 0)] * a.ndim
    widths[axis] = (0, pad)
    return jnp.pad(a, widths)


def kernel(x, norm1_w, w_in, gdn_conv_w, gdn_A_log, gdn_dt_bias, gdn_norm_w, ssm_conv_w, ssm_conv_b,
           ssm_A_log, ssm_dt_bias, ssm_D, ssm_norm_w, w_out, norm2_w, ffn_w_gate, ffn_w_up,
           ffn_conv_w, ffn_conv_b, ffn_w_down, norm_f_w):
    B, L, D = x.shape
    depth = w_in.shape[0]
    gh = gdn_A_log.shape[1]
    sh = ssm_A_log.shape[1]
    gw = gh * GDN_HEAD_DIM
    sw = sh * SSM_HEAD_DIM
    gn = (ssm_conv_w.shape[-1] - sw) // 2
    n_groups = gn // SSM_STATE
    hpg = sh // n_groups
    assert 2 * gh + sh <= LANES and L % CHUNK == 0
    T = B * L
    n_chunks = T // CHUNK
    f_dim = ffn_w_gate.shape[-1]
    f_pad = -(-f_dim // 1024) * 1024 if f_dim >= 1024 else f_dim

    o_gate = 3 * gw
    o_b = o_gate + gw
    o_a = o_b + gh
    o_z = o_a + gh
    o_xbc = o_z + sw
    o_dt = o_xbc + sw + 2 * gn
    z_col0 = o_b
    x_col0 = o_b + sw
    dt_col0 = 2 * gh

    h2 = x.reshape(T, D)
    for i in range(depth):
        wi = w_in[i]
        w_main = jnp.concatenate([wi[:, :o_b], wi[:, o_z:o_dt]], axis=1).astype(BF16)
        w_small = _pad_to(jnp.concatenate([wi[:, o_b:o_z], wi[:, o_dt:]], axis=1), 1, LANES).astype(BF16)
        no_conv = jnp.zeros((SHORT_CONV, gw + sw), F32)
        conv_w = jnp.concatenate([gdn_conv_w[i], no_conv, ssm_conv_w[i]], axis=1)
        conv_b = jnp.concatenate([jnp.zeros((x_col0,), F32), ssm_conv_b[i]])[None, :]

        proj, small = _in_proj(h2, norm1_w[i][None, :], w_main, w_small, conv_w, conv_b,
                               seq=L, q_width=gw, x_col0=x_col0)

        bias_row = _pad_to(jnp.concatenate([jnp.zeros((gh,), F32), gdn_dt_bias[i], ssm_dt_bias[i]])[None, :], 1, LANES)
        alog_row = _pad_to(jnp.concatenate([jnp.zeros((gh,), F32), gdn_A_log[i], ssm_A_log[i]])[None, :], 1, LANES)
        c1, c2, c2t = _prep(small, bias_row, alog_row, gh)
        c2t3 = c2t.reshape(LANES, n_chunks, CHUNK)
        arow_all = c2t[dt_col0:dt_col0 + sh].reshape(n_groups, hpg, n_chunks, CHUNK)
        arow_all = arow_all.transpose(0, 2, 1, 3).reshape(n_groups, n_chunks, hpg * CHUNK)

        y_gdn = _gdn(proj, c1, c2t3, gdn_norm_w[i][None, :],
                     batch=B, seq=L, n_heads=gh, gate_col0=o_gate // GDN_HEAD_DIM)
        y_ssm = _ssd(proj, c1, c2, arow_all, jnp.repeat(ssm_D[i], SSM_HEAD_DIM)[None, :],
                     ssm_norm_w[i][None, :], batch=B, seq=L, n_groups=n_groups, heads_per_group=hpg,
                     z_col0=z_col0, x_col0=x_col0, dt_col0=dt_col0)

        h2 = _out_proj(h2, y_gdn, y_ssm, w_out[i].astype(BF16))

        act = _ffn_up(h2, norm2_w[i][None, :],
                      _pad_to(ffn_w_gate[i], 1, f_pad).astype(BF16),
                      _pad_to(ffn_w_up[i], 1, f_pad).astype(BF16),
                      _pad_to(ffn_conv_w[i], 1, f_pad), _pad_to(ffn_conv_b[i][None, :], 1, f_pad), seq=L)
        h2 = _ffn_down(act, _pad_to(ffn_w_down[i], 0, f_pad).astype(BF16), h2, norm_f_w[None, :],
                       final_norm=(i == depth - 1))
    return h2.reshape(B, L, D)
```

```python
import functools
import math

import jax
import jax.numpy as jnp
from jax import lax
from jax.experimental import pallas as pl
from jax.experimental.pallas import tpu as pltpu

F32 = jnp.float32
BF16 = jnp.bfloat16

EPS = 1e-6
CHUNK = 64
CHUNK_SHIFT = 6
GDN_HEAD_DIM = 128
GDN_HEADS_PER_STEP = 8
GDN_CHUNKS_PER_ITER = 2
SSD_CHUNKS_PER_ITER = 4
INV_BASE = 8
SSM_HEAD_DIM = 64
SSM_HEAD_SHIFT = 6
SSM_STATE = 128
SHORT_CONV = 4
FFN_CONV = 3
LANES = 128
SUBLANES = 8
HALO_BF16 = 2 * SUBLANES
NEG_BIG = -1e30
VMEM_LIMIT = 60 * 1024 * 1024


def _pick(n, candidates):
    for c in candidates:
        if n % c == 0:
            return c
    raise ValueError(f"no tile in {candidates} divides {n}")


def _cparams(*sem):
    return pltpu.CompilerParams(dimension_semantics=sem, vmem_limit_bytes=VMEM_LIMIT)


def _silu(x):
    return x * (1.0 / (1.0 + jnp.exp(-x)))


def _dot(a, b):
    return jnp.dot(a.astype(BF16), b.astype(BF16), preferred_element_type=F32)


def _dot_nt(a, b):
    return lax.dot_general(a.astype(BF16), b.astype(BF16), (((1,), (1,)), ((), ())),
                           preferred_element_type=F32)


def _dot_tn(a, b):
    return lax.dot_general(a.astype(BF16), b.astype(BF16), (((0,), (0,)), ((), ())),
                           preferred_element_type=F32)


def _dot_f32(a, b):
    return jnp.dot(a, b, preferred_element_type=F32, precision=lax.Precision.HIGHEST)


def _rms_rows(x, w):
    return x * lax.rsqrt(jnp.mean(x * x, axis=-1, keepdims=True) + EPS) * w


def _each(f, *lists):
    return [f(*xs) for xs in zip(*lists)]


def _fill_normed(hn_ref, halo_ref, x_ref, nw_ref, first_of_seq):
    nw = nw_ref[...]
    keep = jnp.where(first_of_seq, 0.0, 1.0)
    hn_ref[0:HALO_BF16, :] = (_rms_rows(halo_ref[...], nw) * keep).astype(BF16)
    hn_ref[HALO_BF16:, :] = _rms_rows(x_ref[...], nw).astype(BF16)


def _causal_taps(ye, cw, cb, width, tm):
    y = cb + cw[width - 1:width, :] * ye[HALO_BF16:, :]
    for k in range(width - 1):
        off = HALO_BF16 - (width - 1) + k
        y = y + cw[k:k + 1, :] * ye[off:off + tm, :]
    return y


def _inproj_kernel(x_ref, halo_ref, nw_ref, w_ref, ws_ref, cw_ref, cb_ref, o_ref, os_ref, hn_ref,
                   *, blocks_per_seq, n_q_tiles, n_qkv_tiles, first_xbc_tile):
    tm, tn = o_ref.shape
    i = pl.program_id(0)
    j = pl.program_id(1)

    @pl.when(j == 0)
    def _():
        _fill_normed(hn_ref, halo_ref, x_ref, nw_ref, i % blocks_per_seq == 0)
        os_ref[...] = jnp.dot(hn_ref[HALO_BF16:, :], ws_ref[...], preferred_element_type=F32)

    has_conv = jnp.logical_or(j < n_qkv_tiles, j >= first_xbc_tile)

    @pl.when(jnp.logical_not(has_conv))
    def _():
        o_ref[...] = jnp.dot(hn_ref[HALO_BF16:, :], w_ref[...], preferred_element_type=F32)

    @pl.when(has_conv)
    def _():
        cw = cw_ref[...]
        cb = cb_ref[...]
        is_qk = j < 2 * n_q_tiles
        q_scale = jnp.where(j < n_q_tiles, GDN_HEAD_DIM ** -0.5, 1.0)
        ye = jnp.dot(hn_ref[...], w_ref[...], preferred_element_type=F32)
        for g in range(0, tn, LANES):
            cols = slice(g, g + LANES)
            y = _silu(_causal_taps(ye[:, cols], cw[:, cols], cb[:, cols], SHORT_CONV, tm))
            l2 = lax.rsqrt(jnp.sum(y * y, axis=-1, keepdims=True) + EPS) * q_scale
            o_ref[:, cols] = y * jnp.where(is_qk, l2, 1.0)


def _in_proj(x2, norm_w, w_main, w_small, conv_w, conv_b, *, seq, q_width, x_col0):
    T, D = x2.shape
    N = w_main.shape[1]
    tm = _pick(seq, (512, 256, 128))
    tn = _pick(math.gcd(math.gcd(q_width, x_col0), N), (1024, 512, 256, 128))
    hb = tm // HALO_BF16
    kern = functools.partial(_inproj_kernel, blocks_per_seq=seq // tm, n_q_tiles=q_width // tn,
                             n_qkv_tiles=3 * q_width // tn, first_xbc_tile=x_col0 // tn)
    return pl.pallas_call(
        kern,
        grid=(T // tm, N // tn),
        in_specs=[
            pl.BlockSpec((tm, D), lambda i, j: (i, 0)),
            pl.BlockSpec((HALO_BF16, D), lambda i, j: (jnp.maximum(i * hb - 1, 0), 0)),
            pl.BlockSpec((1, D), lambda i, j: (0, 0)),
            pl.BlockSpec((D, tn), lambda i, j: (0, j)),
            pl.BlockSpec((D, LANES), lambda i, j: (0, 0)),
            pl.BlockSpec((SHORT_CONV, tn), lambda i, j: (0, j)),
            pl.BlockSpec((1, tn), lambda i, j: (0, j)),
        ],
        out_specs=[
            pl.BlockSpec((tm, tn), lambda i, j: (i, j)),
            pl.BlockSpec((tm, LANES), lambda i, j: (i, 0)),
        ],
        out_shape=[jax.ShapeDtypeStruct((T, N), F32), jax.ShapeDtypeStruct((T, LANES), F32)],
        scratch_shapes=[pltpu.VMEM((HALO_BF16 + tm, D), BF16)],
        compiler_params=_cparams("parallel", "arbitrary"),
        name="in_proj",
    )(x2, x2, norm_w, w_main, w_small, conv_w, conv_b)


def _prep_kernel(s_ref, bias_ref, alog_ref, c1_ref, c2_ref, c2t_ref, *, n_beta, n_gdn):
    tb = s_ref.shape[0]
    s = s_ref[...]
    lane = lax.broadcasted_iota(jnp.int32, s.shape, 1)
    xb = s + bias_ref[...]
    sp = jnp.maximum(xb, 0.0) + jnp.log(1.0 + jnp.exp(-jnp.abs(xb)))
    sig = 1.0 / (1.0 + jnp.exp(-s))
    a_neg = -jnp.exp(alog_ref[...])
    step = jnp.where(lane >= n_beta, sp * a_neg, 0.0)
    row = lax.broadcasted_iota(jnp.int32, (tb, tb), 0)
    col = lax.broadcasted_iota(jnp.int32, (tb, tb), 1)
    same_chunk = jnp.right_shift(row, CHUNK_SHIFT) == jnp.right_shift(col, CHUNK_SHIFT)
    tri = jnp.where(same_chunk, jnp.where(row >= col, 1.0, 0.0), 0.0).astype(F32)
    cs = _dot_f32(tri, step)
    c1_ref[...] = jnp.where(lane < n_beta, sig, jnp.where(lane < n_gdn, cs, sp))
    c2_ref[...] = cs
    c2t_ref[...] = cs.T


def _prep(small, bias_row, alog_row, n_gdn_heads):
    T = small.shape[0]
    tb = _pick(T, (256, 128, 64))
    kern = functools.partial(_prep_kernel, n_beta=n_gdn_heads, n_gdn=2 * n_gdn_heads)
    return pl.pallas_call(
        kern,
        grid=(T // tb,),
        in_specs=[
            pl.BlockSpec((tb, LANES), lambda i: (i, 0)),
            pl.BlockSpec((1, LANES), lambda i: (0, 0)),
            pl.BlockSpec((1, LANES), lambda i: (0, 0)),
        ],
        out_specs=[
            pl.BlockSpec((tb, LANES), lambda i: (i, 0)),
            pl.BlockSpec((tb, LANES), lambda i: (i, 0)),
            pl.BlockSpec((LANES, tb), lambda i: (0, i)),
        ],
        out_shape=[
            jax.ShapeDtypeStruct((T, LANES), F32),
            jax.ShapeDtypeStruct((T, LANES), F32),
            jax.ShapeDtypeStruct((LANES, T), F32),
        ],
        compiler_params=_cparams("parallel"),
        name="prep",
    )(small, bias_row, alog_row)


def _pick_col(cc, lane, idx):
    return jnp.sum(jnp.where(lane == idx, cc, 0.0), axis=-1, keepdims=True)


def _inv_unit_lower_many(mats):
    n = mats[0].shape[0]
    ri = lax.broadcasted_iota(jnp.int32, (n, n), 0)
    ci = lax.broadcasted_iota(jnp.int32, (n, n), 1)
    eye = jnp.where(ri == ci, 1.0, 0.0).astype(F32)
    blk = ri ^ ci
    ps = _each(lambda a: jnp.where(blk < INV_BASE, -a, 0.0), mats)
    ts = _each(lambda p: eye + p, ps)
    m = 2
    while m < INV_BASE:
        ps = _each(lambda p: _dot(p, p), ps)
        ts = _each(lambda t, p: t + _dot(t, p), ts, ps)
        m *= 2
    m = INV_BASE
    while m < n:
        offs = _each(lambda a: jnp.where(blk >= m, jnp.where(blk < 2 * m, a, 0.0), 0.0), mats)
        xs = _each(_dot, offs, ts)
        ts = _each(lambda t, x: t - _dot(t, x), ts, xs)
        m *= 2
    return ts


def _gdn_kernel(q_ref, k_ref, v_ref, gate_ref, c1_ref, grow_ref, nw_ref,
                o_ref, state_ref, u_s, w_s, qd_s, kd_s, attn_s, beta_s, gc_s,
                *, n_heads, hb, chunks_per_iter):
    tb = q_ref.shape[0]
    hd = GDN_HEAD_DIM
    h0 = pl.program_id(1) * hb
    n_chunks = tb // CHUNK
    heads = list(range(hb))
    head_cols = [slice(hh * hd, (hh + 1) * hd) for hh in heads]

    @pl.when(pl.program_id(2) == 0)
    def _():
        state_ref[...] = jnp.zeros_like(state_ref)

    cc = c1_ref[...]
    lane = lax.broadcasted_iota(jnp.int32, cc.shape, 1)
    for hh in heads:
        beta_s[hh] = jnp.broadcast_to(_pick_col(cc, lane, h0 + hh), cc.shape)
        gc_s[hh] = jnp.broadcast_to(_pick_col(cc, lane, n_heads + h0 + hh), cc.shape)

    ri = lax.broadcasted_iota(jnp.int32, (CHUNK, CHUNK), 0)
    ci = lax.broadcasted_iota(jnp.int32, (CHUNK, CHUNK), 1)
    causal = ri >= ci
    strict = ri > ci

    def intra_load(c, hh):
        rows = pl.ds(pl.multiple_of(c * CHUNK, CHUNK), CHUNK)
        cols = head_cols[hh]
        return (q_ref[rows, cols], k_ref[rows, cols], v_ref[rows, cols], beta_s[hh, rows, :],
                gc_s[hh, rows, :], grow_ref[hh, pl.ds(c, 1), :])

    def intra_compute(q, k, v, beta, gcol, grow):
        gamma = _each(lambda gc, gr: jnp.exp(jnp.where(causal, gc[:, :CHUNK] - gr, NEG_BIG)), gcol, grow)
        kb = _each(jnp.multiply, k, beta)
        a = _each(lambda kb_, k_, gm: jnp.where(strict, _dot_nt(kb_, k_) * gm, 0.0), kb, k, gamma)
        attn = _each(lambda q_, k_, gm: _dot_nt(q_, k_) * gm, q, k, gamma)
        t_inv = _inv_unit_lower_many(a)
        eg = _each(jnp.exp, gcol)
        u = _each(lambda t, v_, b: _dot(t, v_ * b), t_inv, v, beta)
        w = _each(lambda t, kb_, e: _dot(t, kb_ * e), t_inv, kb, eg)
        q_dec = _each(jnp.multiply, q, eg)
        k_dec = _each(lambda k_, gc: k_ * jnp.exp(gc[CHUNK - 1:CHUNK, :] - gc), k, gcol)
        return list(zip(u, w, attn, q_dec, k_dec))

    def intra_store(c, hh, u, w, attn, q_dec, k_dec):
        rows = pl.ds(pl.multiple_of(c * CHUNK, CHUNK), CHUNK)
        cols = head_cols[hh]
        u_s[rows, cols] = u
        w_s[rows, cols] = w.astype(w_s.dtype)
        attn_s[hh, rows, :] = attn.astype(attn_s.dtype)
        qd_s[rows, cols] = q_dec.astype(qd_s.dtype)
        kd_s[rows, cols] = k_dec.astype(kd_s.dtype)

    def pass1(it, carry):
        work = [(it * chunks_per_iter + cc_, hh) for cc_ in range(chunks_per_iter) for hh in heads]
        loaded = [intra_load(c, hh) for c, hh in work]
        results = intra_compute(*[list(col) for col in zip(*loaded)])
        for (c, hh), res in zip(work, results):
            intra_store(c, hh, *res)
        return carry

    def pass2(c, carry):
        r0 = pl.multiple_of(c * CHUNK, CHUNK)
        rows = pl.ds(r0, CHUNK)
        s = [state_ref[hh] for hh in heads]
        u = [u_s[rows, cols] for cols in head_cols]
        w = [w_s[rows, cols] for cols in head_cols]
        qd = [qd_s[rows, cols] for cols in head_cols]
        kd = [kd_s[rows, cols] for cols in head_cols]
        attn = [attn_s[hh, rows, :] for hh in heads]
        dec = [jnp.exp(gc_s[hh, pl.ds(r0 + CHUNK - 1, 1), :]) for hh in heads]
        gate = [gate_ref[rows, cols] for cols in head_cols]
        ws = _each(_dot, w, s)
        qs = _each(_dot, qd, s)
        v_new = _each(jnp.subtract, u, ws)
        kv = _each(_dot_tn, kd, v_new)
        av = _each(_dot, attn, v_new)
        for hh in heads:
            state_ref[hh] = s[hh] * dec[hh] + kv[hh]
        nw = nw_ref[...]
        for hh in heads:
            o = _rms_rows(qs[hh] + av[hh], nw) * _silu(gate[hh])
            o_ref[rows, head_cols[hh]] = o.astype(o_ref.dtype)
        return carry

    lax.fori_loop(0, n_chunks // chunks_per_iter, pass1, 0)
    lax.fori_loop(0, n_chunks, pass2, 0)


def _gdn(proj, c1, c2t3, norm_w, *, batch, seq, n_heads, gate_col0):
    T = proj.shape[0]
    tb = _pick(seq, (512, 256, 128, 64))
    nt = seq // tb
    hd = GDN_HEAD_DIM
    cpb = tb // CHUNK
    hb = _pick(n_heads, (GDN_HEADS_PER_STEP, 4, 2, 1))
    cpi = _pick(cpb, (GDN_CHUNKS_PER_ITER, 2, 1))
    bw = hb * hd
    ng = n_heads // hb
    assert gate_col0 % hb == 0

    def rows(b, h, t):
        return b * nt + t

    kern = functools.partial(_gdn_kernel, n_heads=n_heads, hb=hb, chunks_per_iter=cpi)
    return pl.pallas_call(
        kern,
        grid=(batch, ng, nt),
        in_specs=[
            pl.BlockSpec((tb, bw), lambda b, h, t: (rows(b, h, t), h)),
            pl.BlockSpec((tb, bw), lambda b, h, t: (rows(b, h, t), ng + h)),
            pl.BlockSpec((tb, bw), lambda b, h, t: (rows(b, h, t), 2 * ng + h)),
            pl.BlockSpec((tb, bw), lambda b, h, t: (rows(b, h, t), gate_col0 // hb + h)),
            pl.BlockSpec((tb, LANES), lambda b, h, t: (rows(b, h, t), 0)),
            pl.BlockSpec((hb, cpb, CHUNK), lambda b, h, t: (ng + h, rows(b, h, t), 0)),
            pl.BlockSpec((1, hd), lambda b, h, t: (0, 0)),
        ],
        out_specs=pl.BlockSpec((tb, bw), lambda b, h, t: (rows(b, h, t), h)),
        out_shape=jax.ShapeDtypeStruct((T, n_heads * hd), BF16),
        scratch_shapes=[
            pltpu.VMEM((hb, hd, hd), F32),
            pltpu.VMEM((tb, bw), F32),
            pltpu.VMEM((tb, bw), BF16),
            pltpu.VMEM((tb, bw), BF16),
            pltpu.VMEM((tb, bw), BF16),
            pltpu.VMEM((hb, tb, CHUNK), BF16),
            pltpu.VMEM((hb, tb, LANES), F32),
            pltpu.VMEM((hb, tb, LANES), F32),
        ],
        compiler_params=_cparams("parallel", "parallel", "arbitrary"),
        name="gdn",
    )(proj, proj, proj, proj, c1, c2t3, norm_w)


def _ssd_kernel(x_ref, bm_ref, cm_ref, z_ref, c1_ref, c2_ref, arow_ref, dskip_ref, nw_ref,
                o_ref, state_ref, dt_s, ax_s, *, dt_col0, heads_per_group, chunks_per_iter):
    tb, width = x_ref.shape
    R = heads_per_group
    g = pl.program_id(1)

    @pl.when(pl.program_id(2) == 0)
    def _():
        state_ref[...] = jnp.zeros_like(state_ref)

    c1 = c1_ref[...]
    c2 = c2_ref[...]
    lane = lax.broadcasted_iota(jnp.int32, c1.shape, 1)
    grp = jnp.right_shift(lax.broadcasted_iota(jnp.int32, (tb, width), 1), SSM_HEAD_SHIFT)
    dtx = jnp.zeros((tb, width), F32)
    ax = jnp.zeros((tb, width), F32)
    for r in range(R):
        idx = dt_col0 + g * R + r
        dtx = jnp.where(grp == r, _pick_col(c1, lane, idx), dtx)
        ax = jnp.where(grp == r, _pick_col(c2, lane, idx), ax)
    dt_s[...] = dtx
    ax_s[...] = ax

    ri = lax.broadcasted_iota(jnp.int32, (CHUNK, width), 0)
    ci = lax.broadcasted_iota(jnp.int32, (CHUNK, width), 1)
    causal = ri >= (ci & (CHUNK - 1))
    grp_c = jnp.right_shift(ci, SSM_HEAD_SHIFT)
    dskip = dskip_ref[...]
    nw = nw_ref[...]

    def body(it, carry):
        cs = [it * chunks_per_iter + k for k in range(chunks_per_iter)]
        rows = [pl.ds(pl.multiple_of(c * CHUNK, CHUNK), CHUNK) for c in cs]
        x = [x_ref[r, :] for r in rows]
        bm = [bm_ref[r, :] for r in rows]
        cm = [cm_ref[r, :] for r in rows]
        a = [ax_s[r, :] for r in rows]
        dt = [dt_s[r, :] for r in rows]
        arow = [arow_ref[0, pl.ds(c, 1), :] for c in cs]
        xdt = _each(jnp.multiply, x, dt)
        alast = _each(lambda a_: a_[CHUNK - 1:CHUNK, :], a)
        cb = _each(lambda cm_, bm_: _dot_nt(cm_, jnp.concatenate([bm_] * R, axis=0)), cm, bm)
        states = _each(lambda bm_, xd, al, a_: _dot_tn(bm_, xd * jnp.exp(al - a_)), bm, xdt, alast, a)
        h = state_ref[...]
        h_before = []
        for k in range(chunks_per_iter):
            h_before.append(h)
            h = h * jnp.exp(alast[k]) + states[k]
        state_ref[...] = h
        lmat = _each(lambda a_, ar: jnp.exp(jnp.where(causal, a_ - ar, NEG_BIG)), a, arow)
        xbd = _each(lambda xd: jnp.concatenate([jnp.where(grp_c == r, xd, 0.0) for r in range(R)], axis=0), xdt)
        ydiag = _each(lambda cb_, lm, xb: _dot(cb_ * lm, xb), cb, lmat, xbd)
        yoff = _each(lambda cm_, hp, a_: _dot(cm_, hp) * jnp.exp(a_), cm, h_before, a)
        for k in range(chunks_per_iter):
            y = ydiag[k] + yoff[k] + x[k] * dskip
            y = y * _silu(z_ref[rows[k], :])
            o_ref[rows[k], :] = _rms_rows(y, nw).astype(o_ref.dtype)
        return carry

    lax.fori_loop(0, tb // CHUNK // chunks_per_iter, body, 0)


def _ssd(proj, c1, c2, arow_all, dskip_x, norm_w, *, batch, seq, n_groups, heads_per_group,
         z_col0, x_col0, dt_col0):
    T = proj.shape[0]
    R = heads_per_group
    width = R * SSM_HEAD_DIM
    ssm_width = n_groups * width
    ns = SSM_STATE
    tb = _pick(seq, (512, 256, 128, 64))
    nt = seq // tb
    cpb = tb // CHUNK
    cpi = _pick(cpb, (SSD_CHUNKS_PER_ITER, 2, 1))
    assert CHUNK == SSM_HEAD_DIM
    assert x_col0 % width == 0 and z_col0 % width == 0
    assert (x_col0 + ssm_width) % ns == 0
    xb0 = x_col0 // width
    zb0 = z_col0 // width
    bb0 = (x_col0 + ssm_width) // ns
    cb0 = bb0 + n_groups

    def rows(b, g, t):
        return b * nt + t

    kern = functools.partial(_ssd_kernel, dt_col0=dt_col0, heads_per_group=R, chunks_per_iter=cpi)
    return pl.pallas_call(
        kern,
        grid=(batch, n_groups, nt),
        in_specs=[
            pl.BlockSpec((tb, width), lambda b, g, t: (rows(b, g, t), xb0 + g)),
            pl.BlockSpec((tb, ns), lambda b, g, t: (rows(b, g, t), bb0 + g)),
            pl.BlockSpec((tb, ns), lambda b, g, t: (rows(b, g, t), cb0 + g)),
            pl.BlockSpec((tb, width), lambda b, g, t: (rows(b, g, t), zb0 + g)),
            pl.BlockSpec((tb, LANES), lambda b, g, t: (rows(b, g, t), 0)),
            pl.BlockSpec((tb, LANES), lambda b, g, t: (rows(b, g, t), 0)),
            pl.BlockSpec((1, cpb, width), lambda b, g, t: (g, rows(b, g, t), 0)),
            pl.BlockSpec((1, width), lambda b, g, t: (0, g)),
            pl.BlockSpec((1, width), lambda b, g, t: (0, g)),
        ],
        out_specs=pl.BlockSpec((tb, width), lambda b, g, t: (rows(b, g, t), g)),
        out_shape=jax.ShapeDtypeStruct((T, ssm_width), BF16),
        scratch_shapes=[
            pltpu.VMEM((ns, width), F32),
            pltpu.VMEM((tb, width), F32),
            pltpu.VMEM((tb, width), F32),
        ],
        compiler_params=_cparams("parallel", "parallel", "arbitrary"),
        name="ssd",
    )(proj, proj, proj, proj, c1, c2, arow_all, dskip_x, norm_w)


def _outproj_kernel(x_ref, yg_ref, ys_ref, wg_ref, ws_ref, o_ref):
    acc = jnp.dot(yg_ref[...], wg_ref[...], preferred_element_type=F32)
    acc = acc + jnp.dot(ys_ref[...], ws_ref[...], preferred_element_type=F32)
    o_ref[...] = x_ref[...] + acc


def _out_proj(x2, y_gdn, y_ssm, w_out):
    T, D = x2.shape
    kg, ks = y_gdn.shape[1], y_ssm.shape[1]
    assert kg % ks == 0
    tm = _pick(T, (512, 256, 128))
    tn = _pick(D, (1024, 512, 256, 128))
    return pl.pallas_call(
        _outproj_kernel,
        grid=(T // tm, D // tn),
        in_specs=[
            pl.BlockSpec((tm, tn), lambda i, j: (i, j)),
            pl.BlockSpec((tm, kg), lambda i, j: (i, 0)),
            pl.BlockSpec((tm, ks), lambda i, j: (i, 0)),
            pl.BlockSpec((kg, tn), lambda i, j: (0, j)),
            pl.BlockSpec((ks, tn), lambda i, j: (kg // ks, j)),
        ],
        out_specs=pl.BlockSpec((tm, tn), lambda i, j: (i, j)),
        out_shape=jax.ShapeDtypeStruct((T, D), F32),
        compiler_params=_cparams("parallel", "arbitrary"),
        name="out_proj",
    )(x2, y_gdn, y_ssm, w_out, w_out)


def _ffnup_kernel(h_ref, halo_ref, nw_ref, wg_ref, wu_ref, cw_ref, cb_ref, o_ref, hn_ref,
                  *, blocks_per_seq):
    tm = h_ref.shape[0]

    @pl.when(pl.program_id(1) == 0)
    def _():
        _fill_normed(hn_ref, halo_ref, h_ref, nw_ref, pl.program_id(0) % blocks_per_seq == 0)

    ge = jnp.dot(hn_ref[...], wg_ref[...], preferred_element_type=F32)
    up = jnp.dot(hn_ref[HALO_BF16:, :], wu_ref[...], preferred_element_type=F32)
    g = _causal_taps(ge, cw_ref[...], cb_ref[...], FFN_CONV, tm)
    o_ref[...] = (_silu(g) * up).astype(o_ref.dtype)


def _ffn_up(h2, norm_w, w_gate, w_up, conv_w, conv_b, *, seq):
    T, D = h2.shape
    F = w_gate.shape[1]
    tm = _pick(seq, (512, 256, 128))
    tn = _pick(F, (512, 256, 128))
    kern = functools.partial(_ffnup_kernel, blocks_per_seq=seq // tm)
    hb = tm // HALO_BF16
    return pl.pallas_call(
        kern,
        grid=(T // tm, F // tn),
        in_specs=[
            pl.BlockSpec((tm, D), lambda i, j: (i, 0)),
            pl.BlockSpec((HALO_BF16, D), lambda i, j: (jnp.maximum(i * hb - 1, 0), 0)),
            pl.BlockSpec((1, D), lambda i, j: (0, 0)),
            pl.BlockSpec((D, tn), lambda i, j: (0, j)),
            pl.BlockSpec((D, tn), lambda i, j: (0, j)),
            pl.BlockSpec((FFN_CONV, tn), lambda i, j: (0, j)),
            pl.BlockSpec((1, tn), lambda i, j: (0, j)),
        ],
        out_specs=pl.BlockSpec((tm, tn), lambda i, j: (i, j)),
        out_shape=jax.ShapeDtypeStruct((T, F), BF16),
        scratch_shapes=[pltpu.VMEM((HALO_BF16 + tm, D), BF16)],
        compiler_params=_cparams("parallel", "arbitrary"),
        name="ffn_up",
    )(h2, h2, norm_w, w_gate, w_up, conv_w, conv_b)


def _ffndown_kernel(a_ref, w_ref, h_ref, nw_ref, o_ref, *, final_norm):
    k = pl.program_id(1)

    @pl.when(k == 0)
    def _():
        o_ref[...] = h_ref[...]

    o_ref[...] += jnp.dot(a_ref[...], w_ref[...], preferred_element_type=F32)

    if final_norm:
        @pl.when(k == pl.num_programs(1) - 1)
        def _():
            o_ref[...] = _rms_rows(o_ref[...], nw_ref[...])


def _ffn_down(act, w_down, h2, norm_w, *, final_norm):
    T, F = act.shape
    D = w_down.shape[1]
    tm = _pick(T, (512, 256, 128))
    tk = _pick(F, (1024, 512, 256, 128))
    return pl.pallas_call(
        functools.partial(_ffndown_kernel, final_norm=final_norm),
        grid=(T // tm, F // tk),
        in_specs=[
            pl.BlockSpec((tm, tk), lambda i, k: (i, k)),
            pl.BlockSpec((tk, D), lambda i, k: (k, 0)),
            pl.BlockSpec((tm, D), lambda i, k: (i, 0), pipeline_mode=pl.Buffered(1)),
            pl.BlockSpec((1, D), lambda i, k: (0, 0)),
        ],
        out_specs=pl.BlockSpec((tm, D), lambda i, k: (i, 0)),
        out_shape=jax.ShapeDtypeStruct((T, D), F32),
        compiler_params=_cparams("parallel", "arbitrary"),
        name="ffn_down",
    )(act, w_down, h2, norm_w)


def _regroup_win_kernel(w_ref, main_ref, small_ref, *, o_b, o_z, o_dt):
    w = w_ref[...]
    tr, n_in = w.shape
    n_small = (o_z - o_b) + (n_in - o_dt)
    main_ref[:, :o_b] = w[:, :o_b].astype(BF16)
    main_ref[:, o_b:] = w[:, o_z:o_dt].astype(BF16)
    small = jnp.concatenate([w[:, o_b:o_z], w[:, o_dt:], jnp.zeros((tr, LANES - n_small), F32)], axis=1)
    small_ref[...] = small.astype(BF16)


def _regroup_w_in(w, *, o_b, o_z, o_dt):
    D, n_in = w.shape
    n_main = o_b + (o_dt - o_z)
    tr = _pick(D, (128, 64, 32, 16))
    kern = functools.partial(_regroup_win_kernel, o_b=o_b, o_z=o_z, o_dt=o_dt)
    return pl.pallas_call(
        kern,
        grid=(D // tr,),
        in_specs=[pl.BlockSpec((tr, n_in), lambda i: (i, 0))],
        out_specs=[pl.BlockSpec((tr, n_main), lambda i: (i, 0)), pl.BlockSpec((tr, LANES), lambda i: (i, 0))],
        out_shape=[jax.ShapeDtypeStruct((D, n_main), BF16), jax.ShapeDtypeStruct((D, LANES), BF16)],
        compiler_params=_cparams("parallel"),
        name="regroup_w_in",
    )(w)


def _cast_pad_cols_kernel(w_ref, o_ref):
    n = w_ref.shape[1]
    o_ref[:, :n] = w_ref[...].astype(BF16)
    if o_ref.shape[1] > n:
        o_ref[:, n:] = jnp.zeros((o_ref.shape[0], o_ref.shape[1] - n), BF16)


def _cast_pad_cols(w, n_pad):
    rows, n = w.shape
    assert n % LANES == 0
    tr = _pick(rows, (128, 64, 32, 16))
    return pl.pallas_call(
        _cast_pad_cols_kernel,
        grid=(rows // tr,),
        in_specs=[pl.BlockSpec((tr, n), lambda i: (i, 0))],
        out_specs=pl.BlockSpec((tr, n_pad), lambda i: (i, 0)),
        out_shape=jax.ShapeDtypeStruct((rows, n_pad), BF16),
        compiler_params=_cparams("parallel"),
        name="cast_pad_cols",
    )(w)


def _cast_pad_rows_kernel(w_ref, o_ref, *, n_in_blocks):
    keep = pl.program_id(0) < n_in_blocks
    o_ref[...] = jnp.where(keep, w_ref[...], 0.0).astype(BF16)


def _cast_pad_rows(w, n_pad):
    n, cols = w.shape
    tr = _pick(math.gcd(n, n_pad), (256, 128, 64, 32, 16))
    n_in_blocks = n // tr
    return pl.pallas_call(
        functools.partial(_cast_pad_rows_kernel, n_in_blocks=n_in_blocks),
        grid=(n_pad // tr,),
        in_specs=[pl.BlockSpec((tr, cols), lambda i: (jnp.minimum(i, n_in_blocks - 1), 0))],
        out_specs=pl.BlockSpec((tr, cols), lambda i: (i, 0)),
        out_shape=jax.ShapeDtypeStruct((n_pad, cols), BF16),
        compiler_params=_cparams("parallel"),
        name="cast_pad_rows",
    )(w)


def _pad_to(a, axis, size):
    pad = size - a.shape[axis]
    if pad == 0:
        return a
    widths = [(0, 0)] * a.ndim
    widths[axis] = (0, pad)
    return jnp.pad(a, widths)


def kernel(x, norm1_w, w_in, gdn_conv_w, gdn_A_log, gdn_dt_bias, gdn_norm_w, ssm_conv_w, ssm_conv_b,
           ssm_A_log, ssm_dt_bias, ssm_D, ssm_norm_w, w_out, norm2_w, ffn_w_gate, ffn_w_up,
           ffn_conv_w, ffn_conv_b, ffn_w_down, norm_f_w):
    B, L, D = x.shape
    depth = w_in.shape[0]
    gh = gdn_A_log.shape[1]
    sh = ssm_A_log.shape[1]
    gw = gh * GDN_HEAD_DIM
    sw = sh * SSM_HEAD_DIM
    gn = (ssm_conv_w.shape[-1] - sw) // 2
    n_groups = gn // SSM_STATE
    hpg = sh // n_groups
    assert 2 * gh + sh <= LANES and L % CHUNK == 0
    T = B * L
    n_chunks = T // CHUNK
    f_dim = ffn_w_gate.shape[-1]
    f_pad = -(-f_dim // 1024) * 1024 if f_dim >= 1024 else f_dim

    o_gate = 3 * gw
    o_b = o_gate + gw
    o_a = o_b + gh
    o_z = o_a + gh
    o_xbc = o_z + sw
    o_dt = o_xbc + sw + 2 * gn
    z_col0 = o_b
    x_col0 = o_b + sw
    dt_col0 = 2 * gh

    h2 = x.reshape(T, D)
    for i in range(depth):
        wi = w_in[i]
        w_main, w_small = _regroup_w_in(wi, o_b=o_b, o_z=o_z, o_dt=o_dt)
        no_conv = jnp.zeros((SHORT_CONV, gw + sw), F32)
        conv_w = jnp.concatenate([gdn_conv_w[i], no_conv, ssm_conv_w[i]], axis=1)
        conv_b = jnp.concatenate([jnp.zeros((x_col0,), F32), ssm_conv_b[i]])[None, :]

        proj, small = _in_proj(h2, norm1_w[i][None, :], w_main, w_small, conv_w, conv_b,
                               seq=L, q_width=gw, x_col0=x_col0)

        bias_row = _pad_to(jnp.concatenate([jnp.zeros((gh,), F32), gdn_dt_bias[i], ssm_dt_bias[i]])[None, :], 1, LANES)
        alog_row = _pad_to(jnp.concatenate([jnp.zeros((gh,), F32), gdn_A_log[i], ssm_A_log[i]])[None, :], 1, LANES)
        c1, c2, c2t = _prep(small, bias_row, alog_row, gh)
        c2t3 = c2t.reshape(LANES, n_chunks, CHUNK)
        arow_all = c2t[dt_col0:dt_col0 + sh].reshape(n_groups, hpg, n_chunks, CHUNK)
        arow_all = arow_all.transpose(0, 2, 1, 3).reshape(n_groups, n_chunks, hpg * CHUNK)

        y_gdn = _gdn(proj, c1, c2t3, gdn_norm_w[i][None, :],
                     batch=B, seq=L, n_heads=gh, gate_col0=o_gate // GDN_HEAD_DIM)
        y_ssm = _ssd(proj, c1, c2, arow_all, jnp.repeat(ssm_D[i], SSM_HEAD_DIM)[None, :],
                     ssm_norm_w[i][None, :], batch=B, seq=L, n_groups=n_groups, heads_per_group=hpg,
                     z_col0=z_col0, x_col0=x_col0, dt_col0=dt_col0)

        h2 = _out_proj(h2, y_gdn, y_ssm, w_out[i].astype(BF16))

        act = _ffn_up(h2, norm2_w[i][None, :],
                      _cast_pad_cols(ffn_w_gate[i], f_pad), _cast_pad_cols(ffn_w_up[i], f_pad),
                      _pad_to(ffn_conv_w[i], 1, f_pad), _pad_to(ffn_conv_b[i][None, :], 1, f_pad), seq=L)
        h2 = _ffn_down(act, _cast_pad_rows(ffn_w_down[i], f_pad), h2, norm_f_w[None, :],
                       final_norm=(i == depth - 1))
    return h2.reshape(B, L, D)
```

```python
import functools
import math

import jax
import jax.numpy as jnp
from jax import lax
from jax.experimental import pallas as pl
from jax.experimental.pallas import tpu as pltpu

F32 = jnp.float32
BF16 = jnp.bfloat16

EPS = 1e-6
CHUNK = 64
CHUNK_SHIFT = 6
GDN_HEAD_DIM = 128
GDN_HEADS_PER_STEP = 8
GDN_CHUNKS_PER_ITER = 2
SSD_CHUNKS_PER_ITER = 4
INV_BASE = 8
SSM_HEAD_DIM = 64
SSM_HEAD_SHIFT = 6
SSM_STATE = 128
SHORT_CONV = 4
FFN_CONV = 3
LANES = 128
SUBLANES = 8
HALO_BF16 = 2 * SUBLANES
NEG_BIG = -1e30
VMEM_LIMIT = 60 * 1024 * 1024


def _pick(n, candidates):
    for c in candidates:
        if n % c == 0:
            return c
    raise ValueError(f"no tile in {candidates} divides {n}")


def _cparams(*sem):
    return pltpu.CompilerParams(dimension_semantics=sem, vmem_limit_bytes=VMEM_LIMIT)


def _silu(x):
    return x * (1.0 / (1.0 + jnp.exp(-x)))


def _dot(a, b):
    return jnp.dot(a.astype(BF16), b.astype(BF16), preferred_element_type=F32)


def _dot_nt(a, b):
    return lax.dot_general(a.astype(BF16), b.astype(BF16), (((1,), (1,)), ((), ())),
                           preferred_element_type=F32)


def _dot_tn(a, b):
    return lax.dot_general(a.astype(BF16), b.astype(BF16), (((0,), (0,)), ((), ())),
                           preferred_element_type=F32)


def _dot_f32(a, b):
    return jnp.dot(a, b, preferred_element_type=F32, precision=lax.Precision.HIGHEST)


def _rms_rows(x, w):
    return x * lax.rsqrt(jnp.mean(x * x, axis=-1, keepdims=True) + EPS) * w


def _each(f, *lists):
    return [f(*xs) for xs in zip(*lists)]


def _fill_normed(hn_ref, halo_ref, x_ref, nw_ref, first_of_seq):
    nw = nw_ref[...]
    keep = jnp.where(first_of_seq, 0.0, 1.0)
    hn_ref[0:HALO_BF16, :] = (_rms_rows(halo_ref[...], nw) * keep).astype(BF16)
    hn_ref[HALO_BF16:, :] = _rms_rows(x_ref[...], nw).astype(BF16)


def _causal_taps(ye, cw, cb, width, tm):
    y = cb + cw[width - 1:width, :] * ye[HALO_BF16:, :]
    for k in range(width - 1):
        off = HALO_BF16 - (width - 1) + k
        y = y + cw[k:k + 1, :] * ye[off:off + tm, :]
    return y


def _inproj_kernel(x_ref, halo_ref, nw_ref, w_ref, ws_ref, cw_ref, cb_ref, o_ref, os_ref, hn_ref,
                   *, blocks_per_seq, n_q_tiles, n_qkv_tiles, first_xbc_tile):
    tm, tn = o_ref.shape
    i = pl.program_id(0)
    j = pl.program_id(1)

    @pl.when(j == 0)
    def _():
        _fill_normed(hn_ref, halo_ref, x_ref, nw_ref, i % blocks_per_seq == 0)
        os_ref[...] = _dot_nt(hn_ref[HALO_BF16:, :], ws_ref[...])

    has_conv = jnp.logical_or(j < n_qkv_tiles, j >= first_xbc_tile)

    @pl.when(jnp.logical_not(has_conv))
    def _():
        o_ref[...] = _dot_nt(hn_ref[HALO_BF16:, :], w_ref[...])

    @pl.when(has_conv)
    def _():
        cw = cw_ref[...]
        cb = cb_ref[...]
        is_qk = j < 2 * n_q_tiles
        q_scale = jnp.where(j < n_q_tiles, GDN_HEAD_DIM ** -0.5, 1.0)
        ye = _dot_nt(hn_ref[...], w_ref[...])
        for g in range(0, tn, LANES):
            cols = slice(g, g + LANES)
            y = _silu(_causal_taps(ye[:, cols], cw[:, cols], cb[:, cols], SHORT_CONV, tm))
            l2 = lax.rsqrt(jnp.sum(y * y, axis=-1, keepdims=True) + EPS) * q_scale
            o_ref[:, cols] = y * jnp.where(is_qk, l2, 1.0)


def _in_proj(x2, norm_w, w_t, w_small_t, conv_w, conv_b, *, seq, q_width, a_width, gap, n_out, x_col0):
    T, D = x2.shape
    tm = _pick(seq, (512, 256, 128))
    tn = _pick(math.gcd(math.gcd(q_width, x_col0), math.gcd(a_width, n_out)), (1024, 512, 256, 128))
    hb = tm // HALO_BF16
    n_a_tiles = a_width // tn
    row_align = math.gcd(tn, gap)
    assert row_align % HALO_BF16 == 0
    kern = functools.partial(_inproj_kernel, blocks_per_seq=seq // tm, n_q_tiles=q_width // tn,
                             n_qkv_tiles=3 * q_width // tn, first_xbc_tile=x_col0 // tn)

    def w_rows(i, j):
        return pl.multiple_of(j * tn + jnp.where(j >= n_a_tiles, gap, 0), row_align), 0

    return pl.pallas_call(
        kern,
        grid=(T // tm, n_out // tn),
        in_specs=[
            pl.BlockSpec((tm, D), lambda i, j: (i, 0)),
            pl.BlockSpec((HALO_BF16, D), lambda i, j: (jnp.maximum(i * hb - 1, 0), 0)),
            pl.BlockSpec((1, D), lambda i, j: (0, 0)),
            pl.BlockSpec((pl.Element(tn), pl.Element(D)), w_rows),
            pl.BlockSpec((LANES, D), lambda i, j: (0, 0)),
            pl.BlockSpec((SHORT_CONV, tn), lambda i, j: (0, j)),
            pl.BlockSpec((1, tn), lambda i, j: (0, j)),
        ],
        out_specs=[
            pl.BlockSpec((tm, tn), lambda i, j: (i, j)),
            pl.BlockSpec((tm, LANES), lambda i, j: (i, 0)),
        ],
        out_shape=[jax.ShapeDtypeStruct((T, n_out), F32), jax.ShapeDtypeStruct((T, LANES), F32)],
        scratch_shapes=[pltpu.VMEM((HALO_BF16 + tm, D), BF16)],
        compiler_params=_cparams("parallel", "arbitrary"),
        name="in_proj",
    )(x2, x2, norm_w, w_t, w_small_t, conv_w, conv_b)


def _prep_kernel(s_ref, bias_ref, alog_ref, c1_ref, c2_ref, c2t_ref, *, n_beta, n_gdn):
    tb = s_ref.shape[0]
    s = s_ref[...]
    lane = lax.broadcasted_iota(jnp.int32, s.shape, 1)
    xb = s + bias_ref[...]
    sp = jnp.maximum(xb, 0.0) + jnp.log(1.0 + jnp.exp(-jnp.abs(xb)))
    sig = 1.0 / (1.0 + jnp.exp(-s))
    a_neg = -jnp.exp(alog_ref[...])
    step = jnp.where(lane >= n_beta, sp * a_neg, 0.0)
    row = lax.broadcasted_iota(jnp.int32, (tb, tb), 0)
    col = lax.broadcasted_iota(jnp.int32, (tb, tb), 1)
    same_chunk = jnp.right_shift(row, CHUNK_SHIFT) == jnp.right_shift(col, CHUNK_SHIFT)
    tri = jnp.where(same_chunk, jnp.where(row >= col, 1.0, 0.0), 0.0).astype(F32)
    cs = _dot_f32(tri, step)
    c1_ref[...] = jnp.where(lane < n_beta, sig, jnp.where(lane < n_gdn, cs, sp))
    c2_ref[...] = cs
    c2t_ref[...] = cs.T


def _prep(small, bias_row, alog_row, n_gdn_heads):
    T = small.shape[0]
    tb = _pick(T, (256, 128, 64))
    kern = functools.partial(_prep_kernel, n_beta=n_gdn_heads, n_gdn=2 * n_gdn_heads)
    return pl.pallas_call(
        kern,
        grid=(T // tb,),
        in_specs=[
            pl.BlockSpec((tb, LANES), lambda i: (i, 0)),
            pl.BlockSpec((1, LANES), lambda i: (0, 0)),
            pl.BlockSpec((1, LANES), lambda i: (0, 0)),
        ],
        out_specs=[
            pl.BlockSpec((tb, LANES), lambda i: (i, 0)),
            pl.BlockSpec((tb, LANES), lambda i: (i, 0)),
            pl.BlockSpec((LANES, tb), lambda i: (0, i)),
        ],
        out_shape=[
            jax.ShapeDtypeStruct((T, LANES), F32),
            jax.ShapeDtypeStruct((T, LANES), F32),
            jax.ShapeDtypeStruct((LANES, T), F32),
        ],
        compiler_params=_cparams("parallel"),
        name="prep",
    )(small, bias_row, alog_row)


def _pick_col(cc, lane, idx):
    return jnp.sum(jnp.where(lane == idx, cc, 0.0), axis=-1, keepdims=True)


def _inv_unit_lower_many(mats):
    n = mats[0].shape[0]
    ri = lax.broadcasted_iota(jnp.int32, (n, n), 0)
    ci = lax.broadcasted_iota(jnp.int32, (n, n), 1)
    eye = jnp.where(ri == ci, 1.0, 0.0).astype(F32)
    blk = ri ^ ci
    ps = _each(lambda a: jnp.where(blk < INV_BASE, -a, 0.0), mats)
    ts = _each(lambda p: eye + p, ps)
    m = 2
    while m < INV_BASE:
        ps = _each(lambda p: _dot(p, p), ps)
        ts = _each(lambda t, p: t + _dot(t, p), ts, ps)
        m *= 2
    m = INV_BASE
    while m < n:
        offs = _each(lambda a: jnp.where(blk >= m, jnp.where(blk < 2 * m, a, 0.0), 0.0), mats)
        xs = _each(_dot, offs, ts)
        ts = _each(lambda t, x: t - _dot(t, x), ts, xs)
        m *= 2
    return ts


def _gdn_kernel(q_ref, k_ref, v_ref, gate_ref, c1_ref, grow_ref, nw_ref,
                o_ref, state_ref, u_s, w_s, qd_s, kd_s, attn_s, beta_s, gc_s,
                *, n_heads, hb, chunks_per_iter):
    tb = q_ref.shape[0]
    hd = GDN_HEAD_DIM
    h0 = pl.program_id(1) * hb
    n_chunks = tb // CHUNK
    heads = list(range(hb))
    head_cols = [slice(hh * hd, (hh + 1) * hd) for hh in heads]

    @pl.when(pl.program_id(2) == 0)
    def _():
        state_ref[...] = jnp.zeros_like(state_ref)

    cc = c1_ref[...]
    lane = lax.broadcasted_iota(jnp.int32, cc.shape, 1)
    for hh in heads:
        beta_s[hh] = jnp.broadcast_to(_pick_col(cc, lane, h0 + hh), cc.shape)
        gc_s[hh] = jnp.broadcast_to(_pick_col(cc, lane, n_heads + h0 + hh), cc.shape)

    ri = lax.broadcasted_iota(jnp.int32, (CHUNK, CHUNK), 0)
    ci = lax.broadcasted_iota(jnp.int32, (CHUNK, CHUNK), 1)
    causal = ri >= ci
    strict = ri > ci

    def intra_load(c, hh):
        rows = pl.ds(pl.multiple_of(c * CHUNK, CHUNK), CHUNK)
        cols = head_cols[hh]
        return (q_ref[rows, cols], k_ref[rows, cols], v_ref[rows, cols], beta_s[hh, rows, :],
                gc_s[hh, rows, :], grow_ref[hh, pl.ds(c, 1), :])

    def intra_compute(q, k, v, beta, gcol, grow):
        gamma = _each(lambda gc, gr: jnp.exp(jnp.where(causal, gc[:, :CHUNK] - gr, NEG_BIG)), gcol, grow)
        kb = _each(jnp.multiply, k, beta)
        a = _each(lambda kb_, k_, gm: jnp.where(strict, _dot_nt(kb_, k_) * gm, 0.0), kb, k, gamma)
        attn = _each(lambda q_, k_, gm: _dot_nt(q_, k_) * gm, q, k, gamma)
        t_inv = _inv_unit_lower_many(a)
        eg = _each(jnp.exp, gcol)
        u = _each(lambda t, v_, b: _dot(t, v_ * b), t_inv, v, beta)
        w = _each(lambda t, kb_, e: _dot(t, kb_ * e), t_inv, kb, eg)
        q_dec = _each(jnp.multiply, q, eg)
        k_dec = _each(lambda k_, gc: k_ * jnp.exp(gc[CHUNK - 1:CHUNK, :] - gc), k, gcol)
        return list(zip(u, w, attn, q_dec, k_dec))

    def intra_store(c, hh, u, w, attn, q_dec, k_dec):
        rows = pl.ds(pl.multiple_of(c * CHUNK, CHUNK), CHUNK)
        cols = head_cols[hh]
        u_s[rows, cols] = u
        w_s[rows, cols] = w.astype(w_s.dtype)
        attn_s[hh, rows, :] = attn.astype(attn_s.dtype)
        qd_s[rows, cols] = q_dec.astype(qd_s.dtype)
        kd_s[rows, cols] = k_dec.astype(kd_s.dtype)

    def pass1(it, carry):
        work = [(it * chunks_per_iter + cc_, hh) for cc_ in range(chunks_per_iter) for hh in heads]
        loaded = [intra_load(c, hh) for c, hh in work]
        results = intra_compute(*[list(col) for col in zip(*loaded)])
        for (c, hh), res in zip(work, results):
            intra_store(c, hh, *res)
        return carry

    def pass2(c, carry):
        r0 = pl.multiple_of(c * CHUNK, CHUNK)
        rows = pl.ds(r0, CHUNK)
        s = [state_ref[hh] for hh in heads]
        u = [u_s[rows, cols] for cols in head_cols]
        w = [w_s[rows, cols] for cols in head_cols]
        qd = [qd_s[rows, cols] for cols in head_cols]
        kd = [kd_s[rows, cols] for cols in head_cols]
        attn = [attn_s[hh, rows, :] for hh in heads]
        dec = [jnp.exp(gc_s[hh, pl.ds(r0 + CHUNK - 1, 1), :]) for hh in heads]
        gate = [gate_ref[rows, cols] for cols in head_cols]
        ws = _each(_dot, w, s)
        qs = _each(_dot, qd, s)
        v_new = _each(jnp.subtract, u, ws)
        kv = _each(_dot_tn, kd, v_new)
        av = _each(_dot, attn, v_new)
        for hh in heads:
            state_ref[hh] = s[hh] * dec[hh] + kv[hh]
        nw = nw_ref[...]
        for hh in heads:
            o = _rms_rows(qs[hh] + av[hh], nw) * _silu(gate[hh])
            o_ref[rows, head_cols[hh]] = o.astype(o_ref.dtype)
        return carry

    lax.fori_loop(0, n_chunks // chunks_per_iter, pass1, 0)
    lax.fori_loop(0, n_chunks, pass2, 0)


def _gdn(proj, c1, c2t3, norm_w, *, batch, seq, n_heads, gate_col0):
    T = proj.shape[0]
    tb = _pick(seq, (512, 256, 128, 64))
    nt = seq // tb
    hd = GDN_HEAD_DIM
    cpb = tb // CHUNK
    hb = _pick(n_heads, (GDN_HEADS_PER_STEP, 4, 2, 1))
    cpi = _pick(cpb, (GDN_CHUNKS_PER_ITER, 2, 1))
    bw = hb * hd
    ng = n_heads // hb
    assert gate_col0 % hb == 0

    def rows(b, h, t):
        return b * nt + t

    kern = functools.partial(_gdn_kernel, n_heads=n_heads, hb=hb, chunks_per_iter=cpi)
    return pl.pallas_call(
        kern,
        grid=(batch, ng, nt),
        in_specs=[
            pl.BlockSpec((tb, bw), lambda b, h, t: (rows(b, h, t), h)),
            pl.BlockSpec((tb, bw), lambda b, h, t: (rows(b, h, t), ng + h)),
            pl.BlockSpec((tb, bw), lambda b, h, t: (rows(b, h, t), 2 * ng + h)),
            pl.BlockSpec((tb, bw), lambda b, h, t: (rows(b, h, t), gate_col0 // hb + h)),
            pl.BlockSpec((tb, LANES), lambda b, h, t: (rows(b, h, t), 0)),
            pl.BlockSpec((hb, cpb, CHUNK), lambda b, h, t: (ng + h, rows(b, h, t), 0)),
            pl.BlockSpec((1, hd), lambda b, h, t: (0, 0)),
        ],
        out_specs=pl.BlockSpec((tb, bw), lambda b, h, t: (rows(b, h, t), h)),
        out_shape=jax.ShapeDtypeStruct((T, n_heads * hd), BF16),
        scratch_shapes=[
            pltpu.VMEM((hb, hd, hd), F32),
            pltpu.VMEM((tb, bw), F32),
            pltpu.VMEM((tb, bw), BF16),
            pltpu.VMEM((tb, bw), BF16),
            pltpu.VMEM((tb, bw), BF16),
            pltpu.VMEM((hb, tb, CHUNK), BF16),
            pltpu.VMEM((hb, tb, LANES), F32),
            pltpu.VMEM((hb, tb, LANES), F32),
        ],
        compiler_params=_cparams("parallel", "parallel", "arbitrary"),
        name="gdn",
    )(proj, proj, proj, proj, c1, c2t3, norm_w)


def _ssd_kernel(x_ref, bm_ref, cm_ref, z_ref, c1_ref, c2_ref, arow_ref, dskip_ref, nw_ref,
                o_ref, state_ref, dt_s, ax_s, *, dt_col0, heads_per_group, chunks_per_iter):
    tb, width = x_ref.shape
    R = heads_per_group
    g = pl.program_id(1)

    @pl.when(pl.program_id(2) == 0)
    def _():
        state_ref[...] = jnp.zeros_like(state_ref)

    c1 = c1_ref[...]
    c2 = c2_ref[...]
    lane = lax.broadcasted_iota(jnp.int32, c1.shape, 1)
    grp = jnp.right_shift(lax.broadcasted_iota(jnp.int32, (tb, width), 1), SSM_HEAD_SHIFT)
    dtx = jnp.zeros((tb, width), F32)
    ax = jnp.zeros((tb, width), F32)
    for r in range(R):
        idx = dt_col0 + g * R + r
        dtx = jnp.where(grp == r, _pick_col(c1, lane, idx), dtx)
        ax = jnp.where(grp == r, _pick_col(c2, lane, idx), ax)
    dt_s[...] = dtx
    ax_s[...] = ax

    ri = lax.broadcasted_iota(jnp.int32, (CHUNK, width), 0)
    ci = lax.broadcasted_iota(jnp.int32, (CHUNK, width), 1)
    causal = ri >= (ci & (CHUNK - 1))
    grp_c = jnp.right_shift(ci, SSM_HEAD_SHIFT)
    dskip = dskip_ref[...]
    nw = nw_ref[...]

    def body(it, carry):
        cs = [it * chunks_per_iter + k for k in range(chunks_per_iter)]
        rows = [pl.ds(pl.multiple_of(c * CHUNK, CHUNK), CHUNK) for c in cs]
        x = [x_ref[r, :] for r in rows]
        bm = [bm_ref[r, :] for r in rows]
        cm = [cm_ref[r, :] for r in rows]
        a = [ax_s[r, :] for r in rows]
        dt = [dt_s[r, :] for r in rows]
        arow = [arow_ref[0, pl.ds(c, 1), :] for c in cs]
        xdt = _each(jnp.multiply, x, dt)
        alast = _each(lambda a_: a_[CHUNK - 1:CHUNK, :], a)
        cb = _each(lambda cm_, bm_: _dot_nt(cm_, jnp.concatenate([bm_] * R, axis=0)), cm, bm)
        states = _each(lambda bm_, xd, al, a_: _dot_tn(bm_, xd * jnp.exp(al - a_)), bm, xdt, alast, a)
        h = state_ref[...]
        h_before = []
        for k in range(chunks_per_iter):
            h_before.append(h)
            h = h * jnp.exp(alast[k]) + states[k]
        state_ref[...] = h
        lmat = _each(lambda a_, ar: jnp.exp(jnp.where(causal, a_ - ar, NEG_BIG)), a, arow)
        xbd = _each(lambda xd: jnp.concatenate([jnp.where(grp_c == r, xd, 0.0) for r in range(R)], axis=0), xdt)
        ydiag = _each(lambda cb_, lm, xb: _dot(cb_ * lm, xb), cb, lmat, xbd)
        yoff = _each(lambda cm_, hp, a_: _dot(cm_, hp) * jnp.exp(a_), cm, h_before, a)
        for k in range(chunks_per_iter):
            y = ydiag[k] + yoff[k] + x[k] * dskip
            y = y * _silu(z_ref[rows[k], :])
            o_ref[rows[k], :] = _rms_rows(y, nw).astype(o_ref.dtype)
        return carry

    lax.fori_loop(0, tb // CHUNK // chunks_per_iter, body, 0)


def _ssd(proj, c1, c2, arow_all, dskip_x, norm_w, *, batch, seq, n_groups, heads_per_group,
         z_col0, x_col0, dt_col0):
    T = proj.shape[0]
    R = heads_per_group
    width = R * SSM_HEAD_DIM
    ssm_width = n_groups * width
    ns = SSM_STATE
    tb = _pick(seq, (512, 256, 128, 64))
    nt = seq // tb
    cpb = tb // CHUNK
    cpi = _pick(cpb, (SSD_CHUNKS_PER_ITER, 2, 1))
    assert CHUNK == SSM_HEAD_DIM
    assert x_col0 % width == 0 and z_col0 % width == 0
    assert (x_col0 + ssm_width) % ns == 0
    xb0 = x_col0 // width
    zb0 = z_col0 // width
    bb0 = (x_col0 + ssm_width) // ns
    cb0 = bb0 + n_groups

    def rows(b, g, t):
        return b * nt + t

    kern = functools.partial(_ssd_kernel, dt_col0=dt_col0, heads_per_group=R, chunks_per_iter=cpi)
    return pl.pallas_call(
        kern,
        grid=(batch, n_groups, nt),
        in_specs=[
            pl.BlockSpec((tb, width), lambda b, g, t: (rows(b, g, t), xb0 + g)),
            pl.BlockSpec((tb, ns), lambda b, g, t: (rows(b, g, t), bb0 + g)),
            pl.BlockSpec((tb, ns), lambda b, g, t: (rows(b, g, t), cb0 + g)),
            pl.BlockSpec((tb, width), lambda b, g, t: (rows(b, g, t), zb0 + g)),
            pl.BlockSpec((tb, LANES), lambda b, g, t: (rows(b, g, t), 0)),
            pl.BlockSpec((tb, LANES), lambda b, g, t: (rows(b, g, t), 0)),
            pl.BlockSpec((1, cpb, width), lambda b, g, t: (g, rows(b, g, t), 0)),
            pl.BlockSpec((1, width), lambda b, g, t: (0, g)),
            pl.BlockSpec((1, width), lambda b, g, t: (0, g)),
        ],
        out_specs=pl.BlockSpec((tb, width), lambda b, g, t: (rows(b, g, t), g)),
        out_shape=jax.ShapeDtypeStruct((T, ssm_width), BF16),
        scratch_shapes=[
            pltpu.VMEM((ns, width), F32),
            pltpu.VMEM((tb, width), F32),
            pltpu.VMEM((tb, width), F32),
        ],
        compiler_params=_cparams("parallel", "parallel", "arbitrary"),
        name="ssd",
    )(proj, proj, proj, proj, c1, c2, arow_all, dskip_x, norm_w)


def _outproj_kernel(x_ref, yg_ref, ys_ref, wg_ref, ws_ref, o_ref):
    acc = jnp.dot(yg_ref[...], wg_ref[...], preferred_element_type=F32)
    acc = acc + jnp.dot(ys_ref[...], ws_ref[...], preferred_element_type=F32)
    o_ref[...] = x_ref[...] + acc


def _out_proj(x2, y_gdn, y_ssm, w_out):
    T, D = x2.shape
    kg, ks = y_gdn.shape[1], y_ssm.shape[1]
    assert kg % ks == 0
    tm = _pick(T, (512, 256, 128))
    tn = _pick(D, (1024, 512, 256, 128))
    return pl.pallas_call(
        _outproj_kernel,
        grid=(T // tm, D // tn),
        in_specs=[
            pl.BlockSpec((tm, tn), lambda i, j: (i, j)),
            pl.BlockSpec((tm, kg), lambda i, j: (i, 0)),
            pl.BlockSpec((tm, ks), lambda i, j: (i, 0)),
            pl.BlockSpec((kg, tn), lambda i, j: (0, j)),
            pl.BlockSpec((ks, tn), lambda i, j: (kg // ks, j)),
        ],
        out_specs=pl.BlockSpec((tm, tn), lambda i, j: (i, j)),
        out_shape=jax.ShapeDtypeStruct((T, D), F32),
        compiler_params=_cparams("parallel", "arbitrary"),
        name="out_proj",
    )(x2, y_gdn, y_ssm, w_out, w_out)


def _ffnup_kernel(h_ref, halo_ref, nw_ref, wg_ref, wu_ref, cw_ref, cb_ref, o_ref, hn_ref,
                  *, blocks_per_seq):
    tm = h_ref.shape[0]

    @pl.when(pl.program_id(1) == 0)
    def _():
        _fill_normed(hn_ref, halo_ref, h_ref, nw_ref, pl.program_id(0) % blocks_per_seq == 0)

    ge = jnp.dot(hn_ref[...], wg_ref[...], preferred_element_type=F32)
    up = jnp.dot(hn_ref[HALO_BF16:, :], wu_ref[...], preferred_element_type=F32)
    g = _causal_taps(ge, cw_ref[...], cb_ref[...], FFN_CONV, tm)
    o_ref[...] = (_silu(g) * up).astype(o_ref.dtype)


def _ffn_up(h2, norm_w, w_gate, w_up, conv_w, conv_b, *, seq):
    T, D = h2.shape
    F = w_gate.shape[1]
    tm = _pick(seq, (512, 256, 128))
    tn = _pick(F, (512, 256, 128))
    kern = functools.partial(_ffnup_kernel, blocks_per_seq=seq // tm)
    hb = tm // HALO_BF16
    return pl.pallas_call(
        kern,
        grid=(T // tm, F // tn),
        in_specs=[
            pl.BlockSpec((tm, D), lambda i, j: (i, 0)),
            pl.BlockSpec((HALO_BF16, D), lambda i, j: (jnp.maximum(i * hb - 1, 0), 0)),
            pl.BlockSpec((1, D), lambda i, j: (0, 0)),
            pl.BlockSpec((D, tn), lambda i, j: (0, j)),
            pl.BlockSpec((D, tn), lambda i, j: (0, j)),
            pl.BlockSpec((FFN_CONV, tn), lambda i, j: (0, j)),
            pl.BlockSpec((1, tn), lambda i, j: (0, j)),
        ],
        out_specs=pl.BlockSpec((tm, tn), lambda i, j: (i, j)),
        out_shape=jax.ShapeDtypeStruct((T, F), BF16),
        scratch_shapes=[pltpu.VMEM((HALO_BF16 + tm, D), BF16)],
        compiler_params=_cparams("parallel", "arbitrary"),
        name="ffn_up",
    )(h2, h2, norm_w, w_gate, w_up, conv_w, conv_b)


def _ffndown_kernel(a_ref, w_ref, h_ref, nw_ref, o_ref, *, final_norm):
    k = pl.program_id(1)

    @pl.when(k == 0)
    def _():
        o_ref[...] = h_ref[...]

    o_ref[...] += jnp.dot(a_ref[...], w_ref[...], preferred_element_type=F32)

    if final_norm:
        @pl.when(k == pl.num_programs(1) - 1)
        def _():
            o_ref[...] = _rms_rows(o_ref[...], nw_ref[...])


def _ffn_down(act, w_down, h2, norm_w, *, final_norm):
    T, F = act.shape
    D = w_down.shape[1]
    tm = _pick(T, (512, 256, 128))
    tk = _pick(F, (1024, 512, 256, 128))
    return pl.pallas_call(
        functools.partial(_ffndown_kernel, final_norm=final_norm),
        grid=(T // tm, F // tk),
        in_specs=[
            pl.BlockSpec((tm, tk), lambda i, k: (i, k)),
            pl.BlockSpec((tk, D), lambda i, k: (k, 0)),
            pl.BlockSpec((tm, D), lambda i, k: (i, 0), pipeline_mode=pl.Buffered(1)),
            pl.BlockSpec((1, D), lambda i, k: (0, 0)),
        ],
        out_specs=pl.BlockSpec((tm, D), lambda i, k: (i, 0)),
        out_shape=jax.ShapeDtypeStruct((T, D), F32),
        compiler_params=_cparams("parallel", "arbitrary"),
        name="ffn_down",
    )(act, w_down, h2, norm_w)


def _cast_pad_cols_kernel(w_ref, o_ref):
    n = w_ref.shape[1]
    o_ref[:, :n] = w_ref[...].astype(BF16)
    if o_ref.shape[1] > n:
        o_ref[:, n:] = jnp.zeros((o_ref.shape[0], o_ref.shape[1] - n), BF16)


def _cast_pad_cols(w, n_pad):
    rows, n = w.shape
    assert n % LANES == 0
    tr = _pick(rows, (128, 64, 32, 16))
    return pl.pallas_call(
        _cast_pad_cols_kernel,
        grid=(rows // tr,),
        in_specs=[pl.BlockSpec((tr, n), lambda i: (i, 0))],
        out_specs=pl.BlockSpec((tr, n_pad), lambda i: (i, 0)),
        out_shape=jax.ShapeDtypeStruct((rows, n_pad), BF16),
        compiler_params=_cparams("parallel"),
        name="cast_pad_cols",
    )(w)


def _cast_pad_rows_kernel(w_ref, o_ref, *, n_in_blocks):
    keep = pl.program_id(0) < n_in_blocks
    o_ref[...] = jnp.where(keep, w_ref[...], 0.0).astype(BF16)


def _cast_pad_rows(w, n_pad):
    n, cols = w.shape
    tr = _pick(math.gcd(n, n_pad), (256, 128, 64, 32, 16))
    n_in_blocks = n // tr
    return pl.pallas_call(
        functools.partial(_cast_pad_rows_kernel, n_in_blocks=n_in_blocks),
        grid=(n_pad // tr,),
        in_specs=[pl.BlockSpec((tr, cols), lambda i: (jnp.minimum(i, n_in_blocks - 1), 0))],
        out_specs=pl.BlockSpec((tr, cols), lambda i: (i, 0)),
        out_shape=jax.ShapeDtypeStruct((n_pad, cols), BF16),
        compiler_params=_cparams("parallel"),
        name="cast_pad_rows",
    )(w)


def _pad_to(a, axis, size):
    pad = size - a.shape[axis]
    if pad == 0:
        return a
    widths = [(0, 0)] * a.ndim
    widths[axis] = (0, pad)
    return jnp.pad(a, widths)


def kernel(x, norm1_w, w_in, gdn_conv_w, gdn_A_log, gdn_dt_bias, gdn_norm_w, ssm_conv_w, ssm_conv_b,
           ssm_A_log, ssm_dt_bias, ssm_D, ssm_norm_w, w_out, norm2_w, ffn_w_gate, ffn_w_up,
           ffn_conv_w, ffn_conv_b, ffn_w_down, norm_f_w):
    B, L, D = x.shape
    depth = w_in.shape[0]
    gh = gdn_A_log.shape[1]
    sh = ssm_A_log.shape[1]
    gw = gh * GDN_HEAD_DIM
    sw = sh * SSM_HEAD_DIM
    gn = (ssm_conv_w.shape[-1] - sw) // 2
    n_groups = gn // SSM_STATE
    hpg = sh // n_groups
    assert 2 * gh + sh <= LANES and L % CHUNK == 0
    T = B * L
    n_chunks = T // CHUNK
    f_dim = ffn_w_gate.shape[-1]
    f_pad = -(-f_dim // 1024) * 1024 if f_dim >= 1024 else f_dim

    o_gate = 3 * gw
    o_b = o_gate + gw
    o_a = o_b + gh
    o_z = o_a + gh
    o_xbc = o_z + sw
    o_dt = o_xbc + sw + 2 * gn
    z_col0 = o_b
    x_col0 = o_b + sw
    dt_col0 = 2 * gh

    h2 = x.reshape(T, D)
    for i in range(depth):
        w_t = jnp.swapaxes(w_in, 1, 2)[i].astype(BF16)
        w_small_t = _pad_to(jnp.concatenate([w_t[o_b:o_z], w_t[o_dt:]], axis=0), 0, LANES)
        no_conv = jnp.zeros((SHORT_CONV, gw + sw), F32)
        conv_w = jnp.concatenate([gdn_conv_w[i], no_conv, ssm_conv_w[i]], axis=1)
        conv_b = jnp.concatenate([jnp.zeros((x_col0,), F32), ssm_conv_b[i]])[None, :]

        proj, small = _in_proj(h2, norm1_w[i][None, :], w_t, w_small_t, conv_w, conv_b, seq=L, q_width=gw,
                               a_width=o_b, gap=o_z - o_b, n_out=o_b + (o_dt - o_z), x_col0=x_col0)

        bias_row = _pad_to(jnp.concatenate([jnp.zeros((gh,), F32), gdn_dt_bias[i], ssm_dt_bias[i]])[None, :], 1, LANES)
        alog_row = _pad_to(jnp.concatenate([jnp.zeros((gh,), F32), gdn_A_log[i], ssm_A_log[i]])[None, :], 1, LANES)
        c1, c2, c2t = _prep(small, bias_row, alog_row, gh)
        c2t3 = c2t.reshape(LANES, n_chunks, CHUNK)
        arow_all = c2t[dt_col0:dt_col0 + sh].reshape(n_groups, hpg, n_chunks, CHUNK)
        arow_all = arow_all.transpose(0, 2, 1, 3).reshape(n_groups, n_chunks, hpg * CHUNK)

        y_gdn = _gdn(proj, c1, c2t3, gdn_norm_w[i][None, :],
                     batch=B, seq=L, n_heads=gh, gate_col0=o_gate // GDN_HEAD_DIM)
        y_ssm = _ssd(proj, c1, c2, arow_all, jnp.repeat(ssm_D[i], SSM_HEAD_DIM)[None, :],
                     ssm_norm_w[i][None, :], batch=B, seq=L, n_groups=n_groups, heads_per_group=hpg,
                     z_col0=z_col0, x_col0=x_col0, dt_col0=dt_col0)

        h2 = _out_proj(h2, y_gdn, y_ssm, w_out[i].astype(BF16))

        act = _ffn_up(h2, norm2_w[i][None, :],
                      _cast_pad_cols(ffn_w_gate[i], f_pad), _cast_pad_cols(ffn_w_up[i], f_pad),
                      _pad_to(ffn_conv_w[i], 1, f_pad), _pad_to(ffn_conv_b[i][None, :], 1, f_pad), seq=L)
        h2 = _ffn_down(act, _cast_pad_rows(ffn_w_down[i], f_pad), h2, norm_f_w[None, :],
                       final_norm=(i == depth - 1))
    return h2.reshape(B, L, D)
```

```python
import functools
import math

import jax
import jax.numpy as jnp
from jax import lax
from jax.experimental import pallas as pl
from jax.experimental.pallas import tpu as pltpu

F32 = jnp.float32
BF16 = jnp.bfloat16

EPS = 1e-6
CHUNK = 64
CHUNK_SHIFT = 6
GDN_HEAD_DIM = 128
GDN_HEADS_PER_STEP = 8
GDN_CHUNKS_PER_ITER = 4
SSD_CHUNKS_PER_ITER = 8
INV_BASE = 8
SSM_HEAD_DIM = 64
SSM_HEAD_SHIFT = 6
SSM_STATE = 128
SHORT_CONV = 4
FFN_CONV = 3
LANES = 128
SUBLANES = 8
HALO_BF16 = 2 * SUBLANES
NEG_BIG = -1e30
VMEM_LIMIT = 60 * 1024 * 1024


def _pick(n, candidates):
    for c in candidates:
        if n % c == 0:
            return c
    raise ValueError(f"no tile in {candidates} divides {n}")


def _cparams(*sem):
    return pltpu.CompilerParams(dimension_semantics=sem, vmem_limit_bytes=VMEM_LIMIT)


def _silu(x):
    return x * (1.0 / (1.0 + jnp.exp(-x)))


def _dot(a, b):
    return jnp.dot(a.astype(BF16), b.astype(BF16), preferred_element_type=F32)


def _dot_nt(a, b):
    return lax.dot_general(a.astype(BF16), b.astype(BF16), (((1,), (1,)), ((), ())),
                           preferred_element_type=F32)


def _dot_tn(a, b):
    return lax.dot_general(a.astype(BF16), b.astype(BF16), (((0,), (0,)), ((), ())),
                           preferred_element_type=F32)


def _dot_f32(a, b):
    return jnp.dot(a, b, preferred_element_type=F32, precision=lax.Precision.HIGHEST)


def _rms_rows(x, w):
    return x * lax.rsqrt(jnp.mean(x * x, axis=-1, keepdims=True) + EPS) * w


def _each(f, *lists):
    return [f(*xs) for xs in zip(*lists)]


def _fill_normed(hn_ref, halo_ref, x_ref, nw_ref, first_of_seq):
    nw = nw_ref[...]
    keep = jnp.where(first_of_seq, 0.0, 1.0)
    hn_ref[0:HALO_BF16, :] = (_rms_rows(halo_ref[...], nw) * keep).astype(BF16)
    hn_ref[HALO_BF16:, :] = _rms_rows(x_ref[...], nw).astype(BF16)


def _causal_taps(ye, cw, cb, width, tm):
    y = cb + cw[width - 1:width, :] * ye[HALO_BF16:, :]
    for k in range(width - 1):
        off = HALO_BF16 - (width - 1) + k
        y = y + cw[k:k + 1, :] * ye[off:off + tm, :]
    return y


def _inproj_kernel(x_ref, halo_ref, nw_ref, w_ref, ws_ref, cw_ref, cb_ref, o_ref, os_ref, hn_ref,
                   *, blocks_per_seq, n_q_tiles, n_qkv_tiles, first_xbc_tile):
    tm, tn = o_ref.shape
    i = pl.program_id(0)
    j = pl.program_id(1)

    @pl.when(j == 0)
    def _():
        _fill_normed(hn_ref, halo_ref, x_ref, nw_ref, i % blocks_per_seq == 0)
        os_ref[...] = _dot_nt(hn_ref[HALO_BF16:, :], ws_ref[...])

    has_conv = jnp.logical_or(j < n_qkv_tiles, j >= first_xbc_tile)

    @pl.when(jnp.logical_not(has_conv))
    def _():
        o_ref[...] = _dot_nt(hn_ref[HALO_BF16:, :], w_ref[...])

    @pl.when(has_conv)
    def _():
        cw = cw_ref[...]
        cb = cb_ref[...]
        is_qk = j < 2 * n_q_tiles
        q_scale = jnp.where(j < n_q_tiles, GDN_HEAD_DIM ** -0.5, 1.0)
        ye = _dot_nt(hn_ref[...], w_ref[...])
        for g in range(0, tn, LANES):
            cols = slice(g, g + LANES)
            y = _silu(_causal_taps(ye[:, cols], cw[:, cols], cb[:, cols], SHORT_CONV, tm))
            l2 = lax.rsqrt(jnp.sum(y * y, axis=-1, keepdims=True) + EPS) * q_scale
            o_ref[:, cols] = y * jnp.where(is_qk, l2, 1.0)


def _in_proj(x2, norm_w, w_t, w_small_t, conv_w, conv_b, *, seq, q_width, a_width, gap, n_out, x_col0):
    T, D = x2.shape
    tm = _pick(seq, (512, 256, 128))
    tn = _pick(math.gcd(math.gcd(q_width, x_col0), math.gcd(a_width, n_out)), (1024, 512, 256, 128))
    hb = tm // HALO_BF16
    n_a_tiles = a_width // tn
    row_align = math.gcd(tn, gap)
    assert row_align % HALO_BF16 == 0
    kern = functools.partial(_inproj_kernel, blocks_per_seq=seq // tm, n_q_tiles=q_width // tn,
                             n_qkv_tiles=3 * q_width // tn, first_xbc_tile=x_col0 // tn)

    def w_rows(i, j):
        return pl.multiple_of(j * tn + jnp.where(j >= n_a_tiles, gap, 0), row_align), 0

    return pl.pallas_call(
        kern,
        grid=(T // tm, n_out // tn),
        in_specs=[
            pl.BlockSpec((tm, D), lambda i, j: (i, 0)),
            pl.BlockSpec((HALO_BF16, D), lambda i, j: (jnp.maximum(i * hb - 1, 0), 0)),
            pl.BlockSpec((1, D), lambda i, j: (0, 0)),
            pl.BlockSpec((pl.Element(tn), pl.Element(D)), w_rows),
            pl.BlockSpec((LANES, D), lambda i, j: (0, 0)),
            pl.BlockSpec((SHORT_CONV, tn), lambda i, j: (0, j)),
            pl.BlockSpec((1, tn), lambda i, j: (0, j)),
        ],
        out_specs=[
            pl.BlockSpec((tm, tn), lambda i, j: (i, j)),
            pl.BlockSpec((tm, LANES), lambda i, j: (i, 0)),
        ],
        out_shape=[jax.ShapeDtypeStruct((T, n_out), F32), jax.ShapeDtypeStruct((T, LANES), F32)],
        scratch_shapes=[pltpu.VMEM((HALO_BF16 + tm, D), BF16)],
        compiler_params=_cparams("parallel", "arbitrary"),
        name="in_proj",
    )(x2, x2, norm_w, w_t, w_small_t, conv_w, conv_b)


def _prep_kernel(s_ref, bias_ref, alog_ref, c1_ref, c2_ref, c2t_ref, *, n_beta, n_gdn):
    tb = s_ref.shape[0]
    s = s_ref[...]
    lane = lax.broadcasted_iota(jnp.int32, s.shape, 1)
    xb = s + bias_ref[...]
    sp = jnp.maximum(xb, 0.0) + jnp.log(1.0 + jnp.exp(-jnp.abs(xb)))
    sig = 1.0 / (1.0 + jnp.exp(-s))
    a_neg = -jnp.exp(alog_ref[...])
    step = jnp.where(lane >= n_beta, sp * a_neg, 0.0)
    row = lax.broadcasted_iota(jnp.int32, (tb, tb), 0)
    col = lax.broadcasted_iota(jnp.int32, (tb, tb), 1)
    same_chunk = jnp.right_shift(row, CHUNK_SHIFT) == jnp.right_shift(col, CHUNK_SHIFT)
    tri = jnp.where(same_chunk, jnp.where(row >= col, 1.0, 0.0), 0.0).astype(F32)
    cs = _dot_f32(tri, step)
    c1_ref[...] = jnp.where(lane < n_beta, sig, jnp.where(lane < n_gdn, cs, sp))
    c2_ref[...] = cs
    c2t_ref[...] = cs.T


def _prep(small, bias_row, alog_row, n_gdn_heads):
    T = small.shape[0]
    tb = _pick(T, (256, 128, 64))
    kern = functools.partial(_prep_kernel, n_beta=n_gdn_heads, n_gdn=2 * n_gdn_heads)
    return pl.pallas_call(
        kern,
        grid=(T // tb,),
        in_specs=[
            pl.BlockSpec((tb, LANES), lambda i: (i, 0)),
            pl.BlockSpec((1, LANES), lambda i: (0, 0)),
            pl.BlockSpec((1, LANES), lambda i: (0, 0)),
        ],
        out_specs=[
            pl.BlockSpec((tb, LANES), lambda i: (i, 0)),
            pl.BlockSpec((tb, LANES), lambda i: (i, 0)),
            pl.BlockSpec((LANES, tb), lambda i: (0, i)),
        ],
        out_shape=[
            jax.ShapeDtypeStruct((T, LANES), F32),
            jax.ShapeDtypeStruct((T, LANES), F32),
            jax.ShapeDtypeStruct((LANES, T), F32),
        ],
        compiler_params=_cparams("parallel"),
        name="prep",
    )(small, bias_row, alog_row)


def _pick_col(cc, lane, idx):
    return jnp.sum(jnp.where(lane == idx, cc, 0.0), axis=-1, keepdims=True)


def _inv_unit_lower_many(mats):
    n = mats[0].shape[0]
    ri = lax.broadcasted_iota(jnp.int32, (n, n), 0)
    ci = lax.broadcasted_iota(jnp.int32, (n, n), 1)
    eye = jnp.where(ri == ci, 1.0, 0.0).astype(F32)
    blk = ri ^ ci
    ps = _each(lambda a: jnp.where(blk < INV_BASE, -a, 0.0), mats)
    ts = _each(lambda p: eye + p, ps)
    m = 2
    while m < INV_BASE:
        ps = _each(lambda p: _dot(p, p), ps)
        ts = _each(lambda t, p: t + _dot(t, p), ts, ps)
        m *= 2
    m = INV_BASE
    while m < n:
        offs = _each(lambda a: jnp.where(blk >= m, jnp.where(blk < 2 * m, a, 0.0), 0.0), mats)
        xs = _each(_dot, offs, ts)
        ts = _each(lambda t, x: t - _dot(t, x), ts, xs)
        m *= 2
    return ts


def _gdn_kernel(q_ref, k_ref, v_ref, gate_ref, c1_ref, grow_ref, nw_ref,
                o_ref, state_ref, u_s, w_s, qd_s, kd_s, attn_s, beta_s, gc_s,
                *, n_heads, hb, chunks_per_iter):
    tb = q_ref.shape[0]
    hd = GDN_HEAD_DIM
    h0 = pl.program_id(1) * hb
    n_chunks = tb // CHUNK
    heads = list(range(hb))
    head_cols = [slice(hh * hd, (hh + 1) * hd) for hh in heads]

    @pl.when(pl.program_id(2) == 0)
    def _():
        state_ref[...] = jnp.zeros_like(state_ref)

    cc = c1_ref[...]
    lane = lax.broadcasted_iota(jnp.int32, cc.shape, 1)
    for hh in heads:
        beta_s[hh] = jnp.broadcast_to(_pick_col(cc, lane, h0 + hh), cc.shape)
        gc_s[hh] = jnp.broadcast_to(_pick_col(cc, lane, n_heads + h0 + hh), cc.shape)

    ri = lax.broadcasted_iota(jnp.int32, (CHUNK, CHUNK), 0)
    ci = lax.broadcasted_iota(jnp.int32, (CHUNK, CHUNK), 1)
    causal = ri >= ci
    strict = ri > ci

    def intra_load(c, hh):
        rows = pl.ds(pl.multiple_of(c * CHUNK, CHUNK), CHUNK)
        cols = head_cols[hh]
        return (q_ref[rows, cols], k_ref[rows, cols], v_ref[rows, cols], beta_s[hh, rows, :],
                gc_s[hh, rows, :], grow_ref[hh, pl.ds(c, 1), :])

    def intra_compute(q, k, v, beta, gcol, grow):
        gamma = _each(lambda gc, gr: jnp.exp(jnp.where(causal, gc[:, :CHUNK] - gr, NEG_BIG)), gcol, grow)
        kb = _each(jnp.multiply, k, beta)
        a = _each(lambda kb_, k_, gm: jnp.where(strict, _dot_nt(kb_, k_) * gm, 0.0), kb, k, gamma)
        attn = _each(lambda q_, k_, gm: _dot_nt(q_, k_) * gm, q, k, gamma)
        t_inv = _inv_unit_lower_many(a)
        eg = _each(jnp.exp, gcol)
        u = _each(lambda t, v_, b: _dot(t, v_ * b), t_inv, v, beta)
        w = _each(lambda t, kb_, e: _dot(t, kb_ * e), t_inv, kb, eg)
        q_dec = _each(jnp.multiply, q, eg)
        k_dec = _each(lambda k_, gc: k_ * jnp.exp(gc[CHUNK - 1:CHUNK, :] - gc), k, gcol)
        return list(zip(u, w, attn, q_dec, k_dec))

    def intra_store(c, hh, u, w, attn, q_dec, k_dec):
        rows = pl.ds(pl.multiple_of(c * CHUNK, CHUNK), CHUNK)
        cols = head_cols[hh]
        u_s[rows, cols] = u
        w_s[rows, cols] = w.astype(w_s.dtype)
        attn_s[hh, rows, :] = attn.astype(attn_s.dtype)
        qd_s[rows, cols] = q_dec.astype(qd_s.dtype)
        kd_s[rows, cols] = k_dec.astype(kd_s.dtype)

    def pass1(it, carry):
        work = [(it * chunks_per_iter + cc_, hh) for cc_ in range(chunks_per_iter) for hh in heads]
        loaded = [intra_load(c, hh) for c, hh in work]
        results = intra_compute(*[list(col) for col in zip(*loaded)])
        for (c, hh), res in zip(work, results):
            intra_store(c, hh, *res)
        return carry

    def pass2(c, carry):
        r0 = pl.multiple_of(c * CHUNK, CHUNK)
        rows = pl.ds(r0, CHUNK)
        s = [state_ref[hh] for hh in heads]
        u = [u_s[rows, cols] for cols in head_cols]
        w = [w_s[rows, cols] for cols in head_cols]
        qd = [qd_s[rows, cols] for cols in head_cols]
        kd = [kd_s[rows, cols] for cols in head_cols]
        attn = [attn_s[hh, rows, :] for hh in heads]
        dec = [jnp.exp(gc_s[hh, pl.ds(r0 + CHUNK - 1, 1), :]) for hh in heads]
        gate = [gate_ref[rows, cols] for cols in head_cols]
        ws = _each(_dot, w, s)
        qs = _each(_dot, qd, s)
        v_new = _each(jnp.subtract, u, ws)
        kv = _each(_dot_tn, kd, v_new)
        av = _each(_dot, attn, v_new)
        for hh in heads:
            state_ref[hh] = s[hh] * dec[hh] + kv[hh]
        nw = nw_ref[...]
        for hh in heads:
            o = _rms_rows(qs[hh] + av[hh], nw) * _silu(gate[hh])
            o_ref[rows, head_cols[hh]] = o.astype(o_ref.dtype)
        return carry

    lax.fori_loop(0, n_chunks // chunks_per_iter, pass1, 0)
    lax.fori_loop(0, n_chunks, pass2, 0)


def _gdn(proj, c1, c2t3, norm_w, *, batch, seq, n_heads, gate_col0):
    T = proj.shape[0]
    tb = _pick(seq, (512, 256, 128, 64))
    nt = seq // tb
    hd = GDN_HEAD_DIM
    cpb = tb // CHUNK
    hb = _pick(n_heads, (GDN_HEADS_PER_STEP, 4, 2, 1))
    cpi = _pick(cpb, (GDN_CHUNKS_PER_ITER, 2, 1))
    bw = hb * hd
    ng = n_heads // hb
    assert gate_col0 % hb == 0

    def rows(b, h, t):
        return b * nt + t

    kern = functools.partial(_gdn_kernel, n_heads=n_heads, hb=hb, chunks_per_iter=cpi)
    return pl.pallas_call(
        kern,
        grid=(batch, ng, nt),
        in_specs=[
            pl.BlockSpec((tb, bw), lambda b, h, t: (rows(b, h, t), h)),
            pl.BlockSpec((tb, bw), lambda b, h, t: (rows(b, h, t), ng + h)),
            pl.BlockSpec((tb, bw), lambda b, h, t: (rows(b, h, t), 2 * ng + h)),
            pl.BlockSpec((tb, bw), lambda b, h, t: (rows(b, h, t), gate_col0 // hb + h)),
            pl.BlockSpec((tb, LANES), lambda b, h, t: (rows(b, h, t), 0)),
            pl.BlockSpec((hb, cpb, CHUNK), lambda b, h, t: (ng + h, rows(b, h, t), 0)),
            pl.BlockSpec((1, hd), lambda b, h, t: (0, 0)),
        ],
        out_specs=pl.BlockSpec((tb, bw), lambda b, h, t: (rows(b, h, t), h)),
        out_shape=jax.ShapeDtypeStruct((T, n_heads * hd), BF16),
        scratch_shapes=[
            pltpu.VMEM((hb, hd, hd), F32),
            pltpu.VMEM((tb, bw), F32),
            pltpu.VMEM((tb, bw), BF16),
            pltpu.VMEM((tb, bw), BF16),
            pltpu.VMEM((tb, bw), BF16),
            pltpu.VMEM((hb, tb, CHUNK), BF16),
            pltpu.VMEM((hb, tb, LANES), F32),
            pltpu.VMEM((hb, tb, LANES), F32),
        ],
        compiler_params=_cparams("parallel", "parallel", "arbitrary"),
        name="gdn",
    )(proj, proj, proj, proj, c1, c2t3, norm_w)


def _ssd_kernel(x_ref, bm_ref, cm_ref, z_ref, c1_ref, c2_ref, arow_ref, dskip_ref, nw_ref,
                o_ref, state_ref, dt_s, ax_s, *, dt_col0, heads_per_group, chunks_per_iter):
    tb, width = x_ref.shape
    R = heads_per_group
    g = pl.program_id(1)

    @pl.when(pl.program_id(2) == 0)
    def _():
        state_ref[...] = jnp.zeros_like(state_ref)

    c1 = c1_ref[...]
    c2 = c2_ref[...]
    lane = lax.broadcasted_iota(jnp.int32, c1.shape, 1)
    grp = jnp.right_shift(lax.broadcasted_iota(jnp.int32, (tb, width), 1), SSM_HEAD_SHIFT)
    dtx = jnp.zeros((tb, width), F32)
    ax = jnp.zeros((tb, width), F32)
    for r in range(R):
        idx = dt_col0 + g * R + r
        dtx = jnp.where(grp == r, _pick_col(c1, lane, idx), dtx)
        ax = jnp.where(grp == r, _pick_col(c2, lane, idx), ax)
    dt_s[...] = dtx
    ax_s[...] = ax

    ri = lax.broadcasted_iota(jnp.int32, (CHUNK, width), 0)
    ci = lax.broadcasted_iota(jnp.int32, (CHUNK, width), 1)
    causal = ri >= (ci & (CHUNK - 1))
    grp_c = jnp.right_shift(ci, SSM_HEAD_SHIFT)
    dskip = dskip_ref[...]
    nw = nw_ref[...]

    def body(it, carry):
        cs = [it * chunks_per_iter + k for k in range(chunks_per_iter)]
        rows = [pl.ds(pl.multiple_of(c * CHUNK, CHUNK), CHUNK) for c in cs]
        x = [x_ref[r, :] for r in rows]
        bm = [bm_ref[r, :] for r in rows]
        cm = [cm_ref[r, :] for r in rows]
        a = [ax_s[r, :] for r in rows]
        dt = [dt_s[r, :] for r in rows]
        arow = [arow_ref[0, pl.ds(c, 1), :] for c in cs]
        xdt = _each(jnp.multiply, x, dt)
        alast = _each(lambda a_: a_[CHUNK - 1:CHUNK, :], a)
        cb = _each(lambda cm_, bm_: _dot_nt(cm_, jnp.concatenate([bm_] * R, axis=0)), cm, bm)
        states = _each(lambda bm_, xd, al, a_: _dot_tn(bm_, xd * jnp.exp(al - a_)), bm, xdt, alast, a)
        h = state_ref[...]
        h_before = []
        for k in range(chunks_per_iter):
            h_before.append(h)
            h = h * jnp.exp(alast[k]) + states[k]
        state_ref[...] = h
        lmat = _each(lambda a_, ar: jnp.exp(jnp.where(causal, a_ - ar, NEG_BIG)), a, arow)
        xbd = _each(lambda xd: jnp.concatenate([jnp.where(grp_c == r, xd, 0.0) for r in range(R)], axis=0), xdt)
        ydiag = _each(lambda cb_, lm, xb: _dot(cb_ * lm, xb), cb, lmat, xbd)
        yoff = _each(lambda cm_, hp, a_: _dot(cm_, hp) * jnp.exp(a_), cm, h_before, a)
        for k in range(chunks_per_iter):
            y = ydiag[k] + yoff[k] + x[k] * dskip
            y = y * _silu(z_ref[rows[k], :])
            o_ref[rows[k], :] = _rms_rows(y, nw).astype(o_ref.dtype)
        return carry

    lax.fori_loop(0, tb // CHUNK // chunks_per_iter, body, 0)


def _ssd(proj, c1, c2, arow_all, dskip_x, norm_w, *, batch, seq, n_groups, heads_per_group,
         z_col0, x_col0, dt_col0):
    T = proj.shape[0]
    R = heads_per_group
    width = R * SSM_HEAD_DIM
    ssm_width = n_groups * width
    ns = SSM_STATE
    tb = _pick(seq, (512, 256, 128, 64))
    nt = seq // tb
    cpb = tb // CHUNK
    cpi = _pick(cpb, (SSD_CHUNKS_PER_ITER, 2, 1))
    assert CHUNK == SSM_HEAD_DIM
    assert x_col0 % width == 0 and z_col0 % width == 0
    assert (x_col0 + ssm_width) % ns == 0
    xb0 = x_col0 // width
    zb0 = z_col0 // width
    bb0 = (x_col0 + ssm_width) // ns
    cb0 = bb0 + n_groups

    def rows(b, g, t):
        return b * nt + t

    kern = functools.partial(_ssd_kernel, dt_col0=dt_col0, heads_per_group=R, chunks_per_iter=cpi)
    return pl.pallas_call(
        kern,
        grid=(batch, n_groups, nt),
        in_specs=[
            pl.BlockSpec((tb, width), lambda b, g, t: (rows(b, g, t), xb0 + g)),
            pl.BlockSpec((tb, ns), lambda b, g, t: (rows(b, g, t), bb0 + g)),
            pl.BlockSpec((tb, ns), lambda b, g, t: (rows(b, g, t), cb0 + g)),
            pl.BlockSpec((tb, width), lambda b, g, t: (rows(b, g, t), zb0 + g)),
            pl.BlockSpec((tb, LANES), lambda b, g, t: (rows(b, g, t), 0)),
            pl.BlockSpec((tb, LANES), lambda b, g, t: (rows(b, g, t), 0)),
            pl.BlockSpec((1, cpb, width), lambda b, g, t: (g, rows(b, g, t), 0)),
            pl.BlockSpec((1, width), lambda b, g, t: (0, g)),
            pl.BlockSpec((1, width), lambda b, g, t: (0, g)),
        ],
        out_specs=pl.BlockSpec((tb, width), lambda b, g, t: (rows(b, g, t), g)),
        out_shape=jax.ShapeDtypeStruct((T, ssm_width), BF16),
        scratch_shapes=[
            pltpu.VMEM((ns, width), F32),
            pltpu.VMEM((tb, width), F32),
            pltpu.VMEM((tb, width), F32),
        ],
        compiler_params=_cparams("parallel", "parallel", "arbitrary"),
        name="ssd",
    )(proj, proj, proj, proj, c1, c2, arow_all, dskip_x, norm_w)


def _outproj_kernel(x_ref, yg_ref, ys_ref, wg_ref, ws_ref, o_ref):
    acc = jnp.dot(yg_ref[...], wg_ref[...], preferred_element_type=F32)
    acc = acc + jnp.dot(ys_ref[...], ws_ref[...], preferred_element_type=F32)
    o_ref[...] = x_ref[...] + acc


def _out_proj(x2, y_gdn, y_ssm, w_out):
    T, D = x2.shape
    kg, ks = y_gdn.shape[1], y_ssm.shape[1]
    assert kg % ks == 0
    tm = _pick(T, (512, 256, 128))
    tn = _pick(D, (1024, 512, 256, 128))
    return pl.pallas_call(
        _outproj_kernel,
        grid=(T // tm, D // tn),
        in_specs=[
            pl.BlockSpec((tm, tn), lambda i, j: (i, j)),
            pl.BlockSpec((tm, kg), lambda i, j: (i, 0)),
            pl.BlockSpec((tm, ks), lambda i, j: (i, 0)),
            pl.BlockSpec((kg, tn), lambda i, j: (0, j)),
            pl.BlockSpec((ks, tn), lambda i, j: (kg // ks, j)),
        ],
        out_specs=pl.BlockSpec((tm, tn), lambda i, j: (i, j)),
        out_shape=jax.ShapeDtypeStruct((T, D), F32),
        compiler_params=_cparams("parallel", "arbitrary"),
        name="out_proj",
    )(x2, y_gdn, y_ssm, w_out, w_out)


def _ffnup_kernel(h_ref, halo_ref, nw_ref, wg_ref, wu_ref, cw_ref, cb_ref, o_ref, hn_ref,
                  *, blocks_per_seq):
    tm = h_ref.shape[0]

    @pl.when(pl.program_id(1) == 0)
    def _():
        _fill_normed(hn_ref, halo_ref, h_ref, nw_ref, pl.program_id(0) % blocks_per_seq == 0)

    ge = jnp.dot(hn_ref[...], wg_ref[...], preferred_element_type=F32)
    up = jnp.dot(hn_ref[HALO_BF16:, :], wu_ref[...], preferred_element_type=F32)
    g = _causal_taps(ge, cw_ref[...], cb_ref[...], FFN_CONV, tm)
    o_ref[...] = (_silu(g) * up).astype(o_ref.dtype)


def _ffn_up(h2, norm_w, w_gate, w_up, conv_w, conv_b, *, seq):
    T, D = h2.shape
    F = w_gate.shape[1]
    tm = _pick(seq, (512, 256, 128))
    tn = _pick(F, (512, 256, 128))
    kern = functools.partial(_ffnup_kernel, blocks_per_seq=seq // tm)
    hb = tm // HALO_BF16
    return pl.pallas_call(
        kern,
        grid=(T // tm, F // tn),
        in_specs=[
            pl.BlockSpec((tm, D), lambda i, j: (i, 0)),
            pl.BlockSpec((HALO_BF16, D), lambda i, j: (jnp.maximum(i * hb - 1, 0), 0)),
            pl.BlockSpec((1, D), lambda i, j: (0, 0)),
            pl.BlockSpec((D, tn), lambda i, j: (0, j)),
            pl.BlockSpec((D, tn), lambda i, j: (0, j)),
            pl.BlockSpec((FFN_CONV, tn), lambda i, j: (0, j)),
            pl.BlockSpec((1, tn), lambda i, j: (0, j)),
        ],
        out_specs=pl.BlockSpec((tm, tn), lambda i, j: (i, j)),
        out_shape=jax.ShapeDtypeStruct((T, F), BF16),
        scratch_shapes=[pltpu.VMEM((HALO_BF16 + tm, D), BF16)],
        compiler_params=_cparams("parallel", "arbitrary"),
        name="ffn_up",
    )(h2, h2, norm_w, w_gate, w_up, conv_w, conv_b)


def _ffndown_kernel(a_ref, w_ref, h_ref, nw_ref, o_ref, *, final_norm):
    k = pl.program_id(1)

    @pl.when(k == 0)
    def _():
        o_ref[...] = h_ref[...]

    o_ref[...] += jnp.dot(a_ref[...], w_ref[...], preferred_element_type=F32)

    if final_norm:
        @pl.when(k == pl.num_programs(1) - 1)
        def _():
            o_ref[...] = _rms_rows(o_ref[...], nw_ref[...])


def _ffn_down(act, w_down, h2, norm_w, *, final_norm):
    T, F = act.shape
    D = w_down.shape[1]
    tm = _pick(T, (512, 256, 128))
    tk = _pick(F, (1024, 512, 256, 128))
    return pl.pallas_call(
        functools.partial(_ffndown_kernel, final_norm=final_norm),
        grid=(T // tm, F // tk),
        in_specs=[
            pl.BlockSpec((tm, tk), lambda i, k: (i, k)),
            pl.BlockSpec((tk, D), lambda i, k: (k, 0)),
            pl.BlockSpec((tm, D), lambda i, k: (i, 0), pipeline_mode=pl.Buffered(1)),
            pl.BlockSpec((1, D), lambda i, k: (0, 0)),
        ],
        out_specs=pl.BlockSpec((tm, D), lambda i, k: (i, 0)),
        out_shape=jax.ShapeDtypeStruct((T, D), F32),
        compiler_params=_cparams("parallel", "arbitrary"),
        name="ffn_down",
    )(act, w_down, h2, norm_w)


def _cast_pad_cols_kernel(w_ref, o_ref):
    n = w_ref.shape[1]
    o_ref[:, :n] = w_ref[...].astype(BF16)
    if o_ref.shape[1] > n:
        o_ref[:, n:] = jnp.zeros((o_ref.shape[0], o_ref.shape[1] - n), BF16)


def _cast_pad_cols(w, n_pad):
    rows, n = w.shape
    assert n % LANES == 0
    tr = _pick(rows, (128, 64, 32, 16))
    return pl.pallas_call(
        _cast_pad_cols_kernel,
        grid=(rows // tr,),
        in_specs=[pl.BlockSpec((tr, n), lambda i: (i, 0))],
        out_specs=pl.BlockSpec((tr, n_pad), lambda i: (i, 0)),
        out_shape=jax.ShapeDtypeStruct((rows, n_pad), BF16),
        compiler_params=_cparams("parallel"),
        name="cast_pad_cols",
    )(w)


def _cast_pad_rows_kernel(w_ref, o_ref, *, n_in_blocks):
    keep = pl.program_id(0) < n_in_blocks
    o_ref[...] = jnp.where(keep, w_ref[...], 0.0).astype(BF16)


def _cast_pad_rows(w, n_pad):
    n, cols = w.shape
    tr = _pick(math.gcd(n, n_pad), (256, 128, 64, 32, 16))
    n_in_blocks = n // tr
    return pl.pallas_call(
        functools.partial(_cast_pad_rows_kernel, n_in_blocks=n_in_blocks),
        grid=(n_pad // tr,),
        in_specs=[pl.BlockSpec((tr, cols), lambda i: (jnp.minimum(i, n_in_blocks - 1), 0))],
        out_specs=pl.BlockSpec((tr, cols), lambda i: (i, 0)),
        out_shape=jax.ShapeDtypeStruct((n_pad, cols), BF16),
        compiler_params=_cparams("parallel"),
        name="cast_pad_rows",
    )(w)


def _pad_to(a, axis, size):
    pad = size - a.shape[axis]
    if pad == 0:
        return a
    widths = [(0, 0)] * a.ndim
    widths[axis] = (0, pad)
    return jnp.pad(a, widths)


def kernel(x, norm1_w, w_in, gdn_conv_w, gdn_A_log, gdn_dt_bias, gdn_norm_w, ssm_conv_w, ssm_conv_b,
           ssm_A_log, ssm_dt_bias, ssm_D, ssm_norm_w, w_out, norm2_w, ffn_w_gate, ffn_w_up,
           ffn_conv_w, ffn_conv_b, ffn_w_down, norm_f_w):
    B, L, D = x.shape
    depth = w_in.shape[0]
    gh = gdn_A_log.shape[1]
    sh = ssm_A_log.shape[1]
    gw = gh * GDN_HEAD_DIM
    sw = sh * SSM_HEAD_DIM
    gn = (ssm_conv_w.shape[-1] - sw) // 2
    n_groups = gn // SSM_STATE
    hpg = sh // n_groups
    assert 2 * gh + sh <= LANES and L % CHUNK == 0
    T = B * L
    n_chunks = T // CHUNK
    f_dim = ffn_w_gate.shape[-1]
    f_pad = -(-f_dim // 1024) * 1024 if f_dim >= 1024 else f_dim

    o_gate = 3 * gw
    o_b = o_gate + gw
    o_a = o_b + gh
    o_z = o_a + gh
    o_xbc = o_z + sw
    o_dt = o_xbc + sw + 2 * gn
    z_col0 = o_b
    x_col0 = o_b + sw
    dt_col0 = 2 * gh

    h2 = x.reshape(T, D)
    for i in range(depth):
        w_t = jnp.swapaxes(w_in, 1, 2)[i].astype(BF16)
        w_small_t = _pad_to(jnp.concatenate([w_t[o_b:o_z], w_t[o_dt:]], axis=0), 0, LANES)
        no_conv = jnp.zeros((SHORT_CONV, gw + sw), F32)
        conv_w = jnp.concatenate([gdn_conv_w[i], no_conv, ssm_conv_w[i]], axis=1)
        conv_b = jnp.concatenate([jnp.zeros((x_col0,), F32), ssm_conv_b[i]])[None, :]

        proj, small = _in_proj(h2, norm1_w[i][None, :], w_t, w_small_t, conv_w, conv_b, seq=L, q_width=gw,
                               a_width=o_b, gap=o_z - o_b, n_out=o_b + (o_dt - o_z), x_col0=x_col0)

        bias_row = _pad_to(jnp.concatenate([jnp.zeros((gh,), F32), gdn_dt_bias[i], ssm_dt_bias[i]])[None, :], 1, LANES)
        alog_row = _pad_to(jnp.concatenate([jnp.zeros((gh,), F32), gdn_A_log[i], ssm_A_log[i]])[None, :], 1, LANES)
        c1, c2, c2t = _prep(small, bias_row, alog_row, gh)
        c2t3 = c2t.reshape(LANES, n_chunks, CHUNK)
        arow_all = c2t[dt_col0:dt_col0 + sh].reshape(n_groups, hpg, n_chunks, CHUNK)
        arow_all = arow_all.transpose(0, 2, 1, 3).reshape(n_groups, n_chunks, hpg * CHUNK)

        y_gdn = _gdn(proj, c1, c2t3, gdn_norm_w[i][None, :],
                     batch=B, seq=L, n_heads=gh, gate_col0=o_gate // GDN_HEAD_DIM)
        y_ssm = _ssd(proj, c1, c2, arow_all, jnp.repeat(ssm_D[i], SSM_HEAD_DIM)[None, :],
                     ssm_norm_w[i][None, :], batch=B, seq=L, n_groups=n_groups, heads_per_group=hpg,
                     z_col0=z_col0, x_col0=x_col0, dt_col0=dt_col0)

        h2 = _out_proj(h2, y_gdn, y_ssm, w_out[i].astype(BF16))

        act = _ffn_up(h2, norm2_w[i][None, :],
                      _cast_pad_cols(ffn_w_gate[i], f_pad), _cast_pad_cols(ffn_w_up[i], f_pad),
                      _pad_to(ffn_conv_w[i], 1, f_pad), _pad_to(ffn_conv_b[i][None, :], 1, f_pad), seq=L)
        h2 = _ffn_down(act, _cast_pad_rows(ffn_w_down[i], f_pad), h2, norm_f_w[None, :],
                       final_norm=(i == depth - 1))
    return h2.reshape(B, L, D)
```

```python
import functools
import math

import jax
import jax.numpy as jnp
from jax import lax
from jax.experimental import pallas as pl
from jax.experimental.pallas import tpu as pltpu

F32 = jnp.float32
BF16 = jnp.bfloat16

EPS = 1e-6
CHUNK = 64
CHUNK_SHIFT = 6
GDN_HEAD_DIM = 128
GDN_HEADS_PER_STEP = 8
GDN_CHUNKS_PER_ITER = 8
SSD_CHUNKS_PER_ITER = 8
INV_BASE = 8
SSM_HEAD_DIM = 64
SSM_HEAD_SHIFT = 6
SSM_STATE = 128
SHORT_CONV = 4
FFN_CONV = 3
LANES = 128
SUBLANES = 8
HALO_BF16 = 2 * SUBLANES
NEG_BIG = -1e30
VMEM_LIMIT = 60 * 1024 * 1024


def _pick(n, candidates):
    for c in candidates:
        if n % c == 0:
            return c
    raise ValueError(f"no tile in {candidates} divides {n}")


def _cparams(*sem):
    return pltpu.CompilerParams(dimension_semantics=sem, vmem_limit_bytes=VMEM_LIMIT)


def _silu(x):
    return x * (1.0 / (1.0 + jnp.exp(-x)))


def _dot(a, b):
    return jnp.dot(a.astype(BF16), b.astype(BF16), preferred_element_type=F32)


def _dot_nt(a, b):
    return lax.dot_general(a.astype(BF16), b.astype(BF16), (((1,), (1,)), ((), ())),
                           preferred_element_type=F32)


def _dot_tn(a, b):
    return lax.dot_general(a.astype(BF16), b.astype(BF16), (((0,), (0,)), ((), ())),
                           preferred_element_type=F32)


def _dot_f32(a, b):
    return jnp.dot(a, b, preferred_element_type=F32, precision=lax.Precision.HIGHEST)


def _rms_rows(x, w):
    return x * lax.rsqrt(jnp.mean(x * x, axis=-1, keepdims=True) + EPS) * w


def _each(f, *lists):
    return [f(*xs) for xs in zip(*lists)]


def _fill_normed(hn_ref, halo_ref, x_ref, nw_ref, first_of_seq):
    nw = nw_ref[...]
    keep = jnp.where(first_of_seq, 0.0, 1.0)
    hn_ref[0:HALO_BF16, :] = (_rms_rows(halo_ref[...], nw) * keep).astype(BF16)
    hn_ref[HALO_BF16:, :] = _rms_rows(x_ref[...], nw).astype(BF16)


def _causal_taps(ye, cw, cb, width, tm):
    y = cb + cw[width - 1:width, :] * ye[HALO_BF16:, :]
    for k in range(width - 1):
        off = HALO_BF16 - (width - 1) + k
        y = y + cw[k:k + 1, :] * ye[off:off + tm, :]
    return y


def _inproj_kernel(x_ref, halo_ref, nw_ref, w_ref, ws_ref, cw_ref, cb_ref, o_ref, os_ref, hn_ref,
                   *, blocks_per_seq, n_q_tiles, n_qkv_tiles, first_xbc_tile):
    tm, tn = o_ref.shape
    i = pl.program_id(0)
    j = pl.program_id(1)

    @pl.when(j == 0)
    def _():
        _fill_normed(hn_ref, halo_ref, x_ref, nw_ref, i % blocks_per_seq == 0)
        os_ref[...] = _dot_nt(hn_ref[HALO_BF16:, :], ws_ref[...])

    has_conv = jnp.logical_or(j < n_qkv_tiles, j >= first_xbc_tile)

    @pl.when(jnp.logical_not(has_conv))
    def _():
        o_ref[...] = _dot_nt(hn_ref[HALO_BF16:, :], w_ref[...])

    @pl.when(has_conv)
    def _():
        cw = cw_ref[...]
        cb = cb_ref[...]
        is_qk = j < 2 * n_q_tiles
        q_scale = jnp.where(j < n_q_tiles, GDN_HEAD_DIM ** -0.5, 1.0)
        ye = _dot_nt(hn_ref[...], w_ref[...])
        for g in range(0, tn, LANES):
            cols = slice(g, g + LANES)
            y = _silu(_causal_taps(ye[:, cols], cw[:, cols], cb[:, cols], SHORT_CONV, tm))
            l2 = lax.rsqrt(jnp.sum(y * y, axis=-1, keepdims=True) + EPS) * q_scale
            o_ref[:, cols] = y * jnp.where(is_qk, l2, 1.0)


def _in_proj(x2, norm_w, w_t, w_small_t, conv_w, conv_b, *, seq, q_width, a_width, gap, n_out, x_col0):
    T, D = x2.shape
    tm = _pick(seq, (512, 256, 128))
    tn = _pick(math.gcd(math.gcd(q_width, x_col0), math.gcd(a_width, n_out)), (1024, 512, 256, 128))
    hb = tm // HALO_BF16
    n_a_tiles = a_width // tn
    row_align = math.gcd(tn, gap)
    assert row_align % HALO_BF16 == 0
    kern = functools.partial(_inproj_kernel, blocks_per_seq=seq // tm, n_q_tiles=q_width // tn,
                             n_qkv_tiles=3 * q_width // tn, first_xbc_tile=x_col0 // tn)

    def w_rows(i, j):
        return pl.multiple_of(j * tn + jnp.where(j >= n_a_tiles, gap, 0), row_align), 0

    return pl.pallas_call(
        kern,
        grid=(T // tm, n_out // tn),
        in_specs=[
            pl.BlockSpec((tm, D), lambda i, j: (i, 0)),
            pl.BlockSpec((HALO_BF16, D), lambda i, j: (jnp.maximum(i * hb - 1, 0), 0)),
            pl.BlockSpec((1, D), lambda i, j: (0, 0)),
            pl.BlockSpec((pl.Element(tn), pl.Element(D)), w_rows),
            pl.BlockSpec((LANES, D), lambda i, j: (0, 0)),
            pl.BlockSpec((SHORT_CONV, tn), lambda i, j: (0, j)),
            pl.BlockSpec((1, tn), lambda i, j: (0, j)),
        ],
        out_specs=[
            pl.BlockSpec((tm, tn), lambda i, j: (i, j)),
            pl.BlockSpec((tm, LANES), lambda i, j: (i, 0)),
        ],
        out_shape=[jax.ShapeDtypeStruct((T, n_out), F32), jax.ShapeDtypeStruct((T, LANES), F32)],
        scratch_shapes=[pltpu.VMEM((HALO_BF16 + tm, D), BF16)],
        compiler_params=_cparams("parallel", "arbitrary"),
        name="in_proj",
    )(x2, x2, norm_w, w_t, w_small_t, conv_w, conv_b)


def _prep_kernel(s_ref, bias_ref, alog_ref, c1_ref, c2_ref, c2t_ref, *, n_beta, n_gdn):
    tb = s_ref.shape[0]
    s = s_ref[...]
    lane = lax.broadcasted_iota(jnp.int32, s.shape, 1)
    xb = s + bias_ref[...]
    sp = jnp.maximum(xb, 0.0) + jnp.log(1.0 + jnp.exp(-jnp.abs(xb)))
    sig = 1.0 / (1.0 + jnp.exp(-s))
    a_neg = -jnp.exp(alog_ref[...])
    step = jnp.where(lane >= n_beta, sp * a_neg, 0.0)
    row = lax.broadcasted_iota(jnp.int32, (tb, tb), 0)
    col = lax.broadcasted_iota(jnp.int32, (tb, tb), 1)
    same_chunk = jnp.right_shift(row, CHUNK_SHIFT) == jnp.right_shift(col, CHUNK_SHIFT)
    tri = jnp.where(same_chunk, jnp.where(row >= col, 1.0, 0.0), 0.0).astype(F32)
    cs = _dot_f32(tri, step)
    c1_ref[...] = jnp.where(lane < n_beta, sig, jnp.where(lane < n_gdn, cs, sp))
    c2_ref[...] = cs
    c2t_ref[...] = cs.T


def _prep(small, bias_row, alog_row, n_gdn_heads):
    T = small.shape[0]
    tb = _pick(T, (256, 128, 64))
    kern = functools.partial(_prep_kernel, n_beta=n_gdn_heads, n_gdn=2 * n_gdn_heads)
    return pl.pallas_call(
        kern,
        grid=(T // tb,),
        in_specs=[
            pl.BlockSpec((tb, LANES), lambda i: (i, 0)),
            pl.BlockSpec((1, LANES), lambda i: (0, 0)),
            pl.BlockSpec((1, LANES), lambda i: (0, 0)),
        ],
        out_specs=[
            pl.BlockSpec((tb, LANES), lambda i: (i, 0)),
            pl.BlockSpec((tb, LANES), lambda i: (i, 0)),
            pl.BlockSpec((LANES, tb), lambda i: (0, i)),
        ],
        out_shape=[
            jax.ShapeDtypeStruct((T, LANES), F32),
            jax.ShapeDtypeStruct((T, LANES), F32),
            jax.ShapeDtypeStruct((LANES, T), F32),
        ],
        compiler_params=_cparams("parallel"),
        name="prep",
    )(small, bias_row, alog_row)


def _pick_col(cc, lane, idx):
    return jnp.sum(jnp.where(lane == idx, cc, 0.0), axis=-1, keepdims=True)


def _inv_unit_lower_many(mats):
    n = mats[0].shape[0]
    ri = lax.broadcasted_iota(jnp.int32, (n, n), 0)
    ci = lax.broadcasted_iota(jnp.int32, (n, n), 1)
    eye = jnp.where(ri == ci, 1.0, 0.0).astype(F32)
    blk = ri ^ ci
    ps = _each(lambda a: jnp.where(blk < INV_BASE, -a, 0.0), mats)
    ts = _each(lambda p: eye + p, ps)
    m = 2
    while m < INV_BASE:
        ps = _each(lambda p: _dot(p, p), ps)
        ts = _each(lambda t, p: t + _dot(t, p), ts, ps)
        m *= 2
    m = INV_BASE
    while m < n:
        offs = _each(lambda a: jnp.where(blk >= m, jnp.where(blk < 2 * m, a, 0.0), 0.0), mats)
        xs = _each(_dot, offs, ts)
        ts = _each(lambda t, x: t - _dot(t, x), ts, xs)
        m *= 2
    return ts


def _gdn_kernel(q_ref, k_ref, v_ref, gate_ref, c1_ref, grow_ref, nw_ref,
                o_ref, state_ref, u_s, w_s, qd_s, kd_s, attn_s, beta_s, gc_s,
                *, n_heads, hb, chunks_per_iter):
    tb = q_ref.shape[0]
    hd = GDN_HEAD_DIM
    h0 = pl.program_id(1) * hb
    n_chunks = tb // CHUNK
    heads = list(range(hb))
    head_cols = [slice(hh * hd, (hh + 1) * hd) for hh in heads]

    @pl.when(pl.program_id(2) == 0)
    def _():
        state_ref[...] = jnp.zeros_like(state_ref)

    cc = c1_ref[...]
    lane = lax.broadcasted_iota(jnp.int32, cc.shape, 1)
    for hh in heads:
        beta_s[hh] = jnp.broadcast_to(_pick_col(cc, lane, h0 + hh), cc.shape)
        gc_s[hh] = jnp.broadcast_to(_pick_col(cc, lane, n_heads + h0 + hh), cc.shape)

    ri = lax.broadcasted_iota(jnp.int32, (CHUNK, CHUNK), 0)
    ci = lax.broadcasted_iota(jnp.int32, (CHUNK, CHUNK), 1)
    causal = ri >= ci
    strict = ri > ci

    def intra_load(c, hh):
        rows = pl.ds(pl.multiple_of(c * CHUNK, CHUNK), CHUNK)
        cols = head_cols[hh]
        return (q_ref[rows, cols], k_ref[rows, cols], v_ref[rows, cols], beta_s[hh, rows, :],
                gc_s[hh, rows, :], grow_ref[hh, pl.ds(c, 1), :])

    def intra_compute(q, k, v, beta, gcol, grow):
        gamma = _each(lambda gc, gr: jnp.exp(jnp.where(causal, gc[:, :CHUNK] - gr, NEG_BIG)), gcol, grow)
        kb = _each(jnp.multiply, k, beta)
        a = _each(lambda kb_, k_, gm: jnp.where(strict, _dot_nt(kb_, k_) * gm, 0.0), kb, k, gamma)
        attn = _each(lambda q_, k_, gm: _dot_nt(q_, k_) * gm, q, k, gamma)
        t_inv = _inv_unit_lower_many(a)
        eg = _each(jnp.exp, gcol)
        u = _each(lambda t, v_, b: _dot(t, v_ * b), t_inv, v, beta)
        w = _each(lambda t, kb_, e: _dot(t, kb_ * e), t_inv, kb, eg)
        q_dec = _each(jnp.multiply, q, eg)
        k_dec = _each(lambda k_, gc: k_ * jnp.exp(gc[CHUNK - 1:CHUNK, :] - gc), k, gcol)
        return list(zip(u, w, attn, q_dec, k_dec))

    def intra_store(c, hh, u, w, attn, q_dec, k_dec):
        rows = pl.ds(pl.multiple_of(c * CHUNK, CHUNK), CHUNK)
        cols = head_cols[hh]
        u_s[rows, cols] = u
        w_s[rows, cols] = w.astype(w_s.dtype)
        attn_s[hh, rows, :] = attn.astype(attn_s.dtype)
        qd_s[rows, cols] = q_dec.astype(qd_s.dtype)
        kd_s[rows, cols] = k_dec.astype(kd_s.dtype)

    def pass1(it, carry):
        work = [(it * chunks_per_iter + cc_, hh) for cc_ in range(chunks_per_iter) for hh in heads]
        loaded = [intra_load(c, hh) for c, hh in work]
        results = intra_compute(*[list(col) for col in zip(*loaded)])
        for (c, hh), res in zip(work, results):
            intra_store(c, hh, *res)
        return carry

    def pass2(c, carry):
        r0 = c * CHUNK
        rows = pl.ds(r0, CHUNK)
        s = [state_ref[hh] for hh in heads]
        u = [u_s[rows, cols] for cols in head_cols]
        w = [w_s[rows, cols] for cols in head_cols]
        qd = [qd_s[rows, cols] for cols in head_cols]
        kd = [kd_s[rows, cols] for cols in head_cols]
        attn = [attn_s[hh, rows, :] for hh in heads]
        dec = [jnp.exp(gc_s[hh, pl.ds(r0 + CHUNK - 1, 1), :]) for hh in heads]
        gate = [gate_ref[rows, cols] for cols in head_cols]
        ws = _each(_dot, w, s)
        qs = _each(_dot, qd, s)
        v_new = _each(jnp.subtract, u, ws)
        kv = _each(_dot_tn, kd, v_new)
        av = _each(_dot, attn, v_new)
        for hh in heads:
            state_ref[hh] = s[hh] * dec[hh] + kv[hh]
        nw = nw_ref[...]
        for hh in heads:
            o = _rms_rows(qs[hh] + av[hh], nw) * _silu(gate[hh])
            o_ref[rows, head_cols[hh]] = o.astype(o_ref.dtype)
        return carry

    lax.fori_loop(0, n_chunks // chunks_per_iter, pass1, 0)
    for c in range(n_chunks):
        pass2(c, 0)


def _gdn(proj, c1, c2t3, norm_w, *, batch, seq, n_heads, gate_col0):
    T = proj.shape[0]
    tb = _pick(seq, (512, 256, 128, 64))
    nt = seq // tb
    hd = GDN_HEAD_DIM
    cpb = tb // CHUNK
    hb = _pick(n_heads, (GDN_HEADS_PER_STEP, 4, 2, 1))
    cpi = _pick(cpb, (GDN_CHUNKS_PER_ITER, 2, 1))
    bw = hb * hd
    ng = n_heads // hb
    assert gate_col0 % hb == 0

    def rows(b, h, t):
        return b * nt + t

    kern = functools.partial(_gdn_kernel, n_heads=n_heads, hb=hb, chunks_per_iter=cpi)
    return pl.pallas_call(
        kern,
        grid=(batch, ng, nt),
        in_specs=[
            pl.BlockSpec((tb, bw), lambda b, h, t: (rows(b, h, t), h)),
            pl.BlockSpec((tb, bw), lambda b, h, t: (rows(b, h, t), ng + h)),
            pl.BlockSpec((tb, bw), lambda b, h, t: (rows(b, h, t), 2 * ng + h)),
            pl.BlockSpec((tb, bw), lambda b, h, t: (rows(b, h, t), gate_col0 // hb + h)),
            pl.BlockSpec((tb, LANES), lambda b, h, t: (rows(b, h, t), 0)),
            pl.BlockSpec((hb, cpb, CHUNK), lambda b, h, t: (ng + h, rows(b, h, t), 0)),
            pl.BlockSpec((1, hd), lambda b, h, t: (0, 0)),
        ],
        out_specs=pl.BlockSpec((tb, bw), lambda b, h, t: (rows(b, h, t), h)),
        out_shape=jax.ShapeDtypeStruct((T, n_heads * hd), BF16),
        scratch_shapes=[
            pltpu.VMEM((hb, hd, hd), F32),
            pltpu.VMEM((tb, bw), F32),
            pltpu.VMEM((tb, bw), BF16),
            pltpu.VMEM((tb, bw), BF16),
            pltpu.VMEM((tb, bw), BF16),
            pltpu.VMEM((hb, tb, CHUNK), BF16),
            pltpu.VMEM((hb, tb, LANES), F32),
            pltpu.VMEM((hb, tb, LANES), F32),
        ],
        compiler_params=_cparams("parallel", "parallel", "arbitrary"),
        name="gdn",
    )(proj, proj, proj, proj, c1, c2t3, norm_w)


def _ssd_kernel(x_ref, bm_ref, cm_ref, z_ref, c1_ref, c2_ref, arow_ref, dskip_ref, nw_ref,
                o_ref, state_ref, dt_s, ax_s, *, dt_col0, heads_per_group, chunks_per_iter):
    tb, width = x_ref.shape
    R = heads_per_group
    g = pl.program_id(1)

    @pl.when(pl.program_id(2) == 0)
    def _():
        state_ref[...] = jnp.zeros_like(state_ref)

    c1 = c1_ref[...]
    c2 = c2_ref[...]
    lane = lax.broadcasted_iota(jnp.int32, c1.shape, 1)
    grp = jnp.right_shift(lax.broadcasted_iota(jnp.int32, (tb, width), 1), SSM_HEAD_SHIFT)
    dtx = jnp.zeros((tb, width), F32)
    ax = jnp.zeros((tb, width), F32)
    for r in range(R):
        idx = dt_col0 + g * R + r
        dtx = jnp.where(grp == r, _pick_col(c1, lane, idx), dtx)
        ax = jnp.where(grp == r, _pick_col(c2, lane, idx), ax)
    dt_s[...] = dtx
    ax_s[...] = ax

    ri = lax.broadcasted_iota(jnp.int32, (CHUNK, width), 0)
    ci = lax.broadcasted_iota(jnp.int32, (CHUNK, width), 1)
    causal = ri >= (ci & (CHUNK - 1))
    grp_c = jnp.right_shift(ci, SSM_HEAD_SHIFT)
    dskip = dskip_ref[...]
    nw = nw_ref[...]

    def body(it, carry):
        cs = [it * chunks_per_iter + k for k in range(chunks_per_iter)]
        rows = [pl.ds(pl.multiple_of(c * CHUNK, CHUNK), CHUNK) for c in cs]
        x = [x_ref[r, :] for r in rows]
        bm = [bm_ref[r, :] for r in rows]
        cm = [cm_ref[r, :] for r in rows]
        a = [ax_s[r, :] for r in rows]
        dt = [dt_s[r, :] for r in rows]
        arow = [arow_ref[0, pl.ds(c, 1), :] for c in cs]
        xdt = _each(jnp.multiply, x, dt)
        alast = _each(lambda a_: a_[CHUNK - 1:CHUNK, :], a)
        cb = _each(lambda cm_, bm_: _dot_nt(cm_, jnp.concatenate([bm_] * R, axis=0)), cm, bm)
        states = _each(lambda bm_, xd, al, a_: _dot_tn(bm_, xd * jnp.exp(al - a_)), bm, xdt, alast, a)
        h = state_ref[...]
        h_before = []
        for k in range(chunks_per_iter):
            h_before.append(h)
            h = h * jnp.exp(alast[k]) + states[k]
        state_ref[...] = h
        lmat = _each(lambda a_, ar: jnp.exp(jnp.where(causal, a_ - ar, NEG_BIG)), a, arow)
        xbd = _each(lambda xd: jnp.concatenate([jnp.where(grp_c == r, xd, 0.0) for r in range(R)], axis=0), xdt)
        ydiag = _each(lambda cb_, lm, xb: _dot(cb_ * lm, xb), cb, lmat, xbd)
        yoff = _each(lambda cm_, hp, a_: _dot(cm_, hp) * jnp.exp(a_), cm, h_before, a)
        for k in range(chunks_per_iter):
            y = ydiag[k] + yoff[k] + x[k] * dskip
            y = y * _silu(z_ref[rows[k], :])
            o_ref[rows[k], :] = _rms_rows(y, nw).astype(o_ref.dtype)
        return carry

    lax.fori_loop(0, tb // CHUNK // chunks_per_iter, body, 0)


def _ssd(proj, c1, c2, arow_all, dskip_x, norm_w, *, batch, seq, n_groups, heads_per_group,
         z_col0, x_col0, dt_col0):
    T = proj.shape[0]
    R = heads_per_group
    width = R * SSM_HEAD_DIM
    ssm_width = n_groups * width
    ns = SSM_STATE
    tb = _pick(seq, (512, 256, 128, 64))
    nt = seq // tb
    cpb = tb // CHUNK
    cpi = _pick(cpb, (SSD_CHUNKS_PER_ITER, 2, 1))
    assert CHUNK == SSM_HEAD_DIM
    assert x_col0 % width == 0 and z_col0 % width == 0
    assert (x_col0 + ssm_width) % ns == 0
    xb0 = x_col0 // width
    zb0 = z_col0 // width
    bb0 = (x_col0 + ssm_width) // ns
    cb0 = bb0 + n_groups

    def rows(b, g, t):
        return b * nt + t

    kern = functools.partial(_ssd_kernel, dt_col0=dt_col0, heads_per_group=R, chunks_per_iter=cpi)
    return pl.pallas_call(
        kern,
        grid=(batch, n_groups, nt),
        in_specs=[
            pl.BlockSpec((tb, width), lambda b, g, t: (rows(b, g, t), xb0 + g)),
            pl.BlockSpec((tb, ns), lambda b, g, t: (rows(b, g, t), bb0 + g)),
            pl.BlockSpec((tb, ns), lambda b, g, t: (rows(b, g, t), cb0 + g)),
            pl.BlockSpec((tb, width), lambda b, g, t: (rows(b, g, t), zb0 + g)),
            pl.BlockSpec((tb, LANES), lambda b, g, t: (rows(b, g, t), 0)),
            pl.BlockSpec((tb, LANES), lambda b, g, t: (rows(b, g, t), 0)),
            pl.BlockSpec((1, cpb, width), lambda b, g, t: (g, rows(b, g, t), 0)),
            pl.BlockSpec((1, width), lambda b, g, t: (0, g)),
            pl.BlockSpec((1, width), lambda b, g, t: (0, g)),
        ],
        out_specs=pl.BlockSpec((tb, width), lambda b, g, t: (rows(b, g, t), g)),
        out_shape=jax.ShapeDtypeStruct((T, ssm_width), BF16),
        scratch_shapes=[
            pltpu.VMEM((ns, width), F32),
            pltpu.VMEM((tb, width), F32),
            pltpu.VMEM((tb, width), F32),
        ],
        compiler_params=_cparams("parallel", "parallel", "arbitrary"),
        name="ssd",
    )(proj, proj, proj, proj, c1, c2, arow_all, dskip_x, norm_w)


def _outproj_kernel(x_ref, yg_ref, ys_ref, wg_ref, ws_ref, o_ref):
    acc = jnp.dot(yg_ref[...], wg_ref[...], preferred_element_type=F32)
    acc = acc + jnp.dot(ys_ref[...], ws_ref[...], preferred_element_type=F32)
    o_ref[...] = x_ref[...] + acc


def _out_proj(x2, y_gdn, y_ssm, w_out):
    T, D = x2.shape
    kg, ks = y_gdn.shape[1], y_ssm.shape[1]
    assert kg % ks == 0
    tm = _pick(T, (512, 256, 128))
    tn = _pick(D, (1024, 512, 256, 128))
    return pl.pallas_call(
        _outproj_kernel,
        grid=(T // tm, D // tn),
        in_specs=[
            pl.BlockSpec((tm, tn), lambda i, j: (i, j)),
            pl.BlockSpec((tm, kg), lambda i, j: (i, 0)),
            pl.BlockSpec((tm, ks), lambda i, j: (i, 0)),
            pl.BlockSpec((kg, tn), lambda i, j: (0, j)),
            pl.BlockSpec((ks, tn), lambda i, j: (kg // ks, j)),
        ],
        out_specs=pl.BlockSpec((tm, tn), lambda i, j: (i, j)),
        out_shape=jax.ShapeDtypeStruct((T, D), F32),
        compiler_params=_cparams("parallel", "arbitrary"),
        name="out_proj",
    )(x2, y_gdn, y_ssm, w_out, w_out)


def _ffnup_kernel(h_ref, halo_ref, nw_ref, wg_ref, wu_ref, cw_ref, cb_ref, o_ref, hn_ref,
                  *, blocks_per_seq):
    tm = h_ref.shape[0]

    @pl.when(pl.program_id(1) == 0)
    def _():
        _fill_normed(hn_ref, halo_ref, h_ref, nw_ref, pl.program_id(0) % blocks_per_seq == 0)

    ge = jnp.dot(hn_ref[...], wg_ref[...], preferred_element_type=F32)
    up = jnp.dot(hn_ref[HALO_BF16:, :], wu_ref[...], preferred_element_type=F32)
    g = _causal_taps(ge, cw_ref[...], cb_ref[...], FFN_CONV, tm)
    o_ref[...] = (_silu(g) * up).astype(o_ref.dtype)


def _ffn_up(h2, norm_w, w_gate, w_up, conv_w, conv_b, *, seq):
    T, D = h2.shape
    F = w_gate.shape[1]
    tm = _pick(seq, (512, 256, 128))
    tn = _pick(F, (512, 256, 128))
    kern = functools.partial(_ffnup_kernel, blocks_per_seq=seq // tm)
    hb = tm // HALO_BF16
    return pl.pallas_call(
        kern,
        grid=(T // tm, F // tn),
        in_specs=[
            pl.BlockSpec((tm, D), lambda i, j: (i, 0)),
            pl.BlockSpec((HALO_BF16, D), lambda i, j: (jnp.maximum(i * hb - 1, 0), 0)),
            pl.BlockSpec((1, D), lambda i, j: (0, 0)),
            pl.BlockSpec((D, tn), lambda i, j: (0, j)),
            pl.BlockSpec((D, tn), lambda i, j: (0, j)),
            pl.BlockSpec((FFN_CONV, tn), lambda i, j: (0, j)),
            pl.BlockSpec((1, tn), lambda i, j: (0, j)),
        ],
        out_specs=pl.BlockSpec((tm, tn), lambda i, j: (i, j)),
        out_shape=jax.ShapeDtypeStruct((T, F), BF16),
        scratch_shapes=[pltpu.VMEM((HALO_BF16 + tm, D), BF16)],
        compiler_params=_cparams("parallel", "arbitrary"),
        name="ffn_up",
    )(h2, h2, norm_w, w_gate, w_up, conv_w, conv_b)


def _ffndown_kernel(a_ref, w_ref, h_ref, nw_ref, o_ref, *, final_norm):
    k = pl.program_id(1)

    @pl.when(k == 0)
    def _():
        o_ref[...] = h_ref[...]

    o_ref[...] += jnp.dot(a_ref[...], w_ref[...], preferred_element_type=F32)

    if final_norm:
        @pl.when(k == pl.num_programs(1) - 1)
        def _():
            o_ref[...] = _rms_rows(o_ref[...], nw_ref[...])


def _ffn_down(act, w_down, h2, norm_w, *, final_norm):
    T, F = act.shape
    D = w_down.shape[1]
    tm = _pick(T, (512, 256, 128))
    tk = _pick(F, (1024, 512, 256, 128))
    return pl.pallas_call(
        functools.partial(_ffndown_kernel, final_norm=final_norm),
        grid=(T // tm, F // tk),
        in_specs=[
            pl.BlockSpec((tm, tk), lambda i, k: (i, k)),
            pl.BlockSpec((tk, D), lambda i, k: (k, 0)),
            pl.BlockSpec((tm, D), lambda i, k: (i, 0), pipeline_mode=pl.Buffered(1)),
            pl.BlockSpec((1, D), lambda i, k: (0, 0)),
        ],
        out_specs=pl.BlockSpec((tm, D), lambda i, k: (i, 0)),
        out_shape=jax.ShapeDtypeStruct((T, D), F32),
        compiler_params=_cparams("parallel", "arbitrary"),
        name="ffn_down",
    )(act, w_down, h2, norm_w)


def _cast_pad_cols_kernel(w_ref, o_ref):
    n = w_ref.shape[1]
    o_ref[:, :n] = w_ref[...].astype(BF16)
    if o_ref.shape[1] > n:
        o_ref[:, n:] = jnp.zeros((o_ref.shape[0], o_ref.shape[1] - n), BF16)


def _cast_pad_cols(w, n_pad):
    rows, n = w.shape
    assert n % LANES == 0
    tr = _pick(rows, (128, 64, 32, 16))
    return pl.pallas_call(
        _cast_pad_cols_kernel,
        grid=(rows // tr,),
        in_specs=[pl.BlockSpec((tr, n), lambda i: (i, 0))],
        out_specs=pl.BlockSpec((tr, n_pad), lambda i: (i, 0)),
        out_shape=jax.ShapeDtypeStruct((rows, n_pad), BF16),
        compiler_params=_cparams("parallel"),
        name="cast_pad_cols",
    )(w)


def _cast_pad_rows_kernel(w_ref, o_ref, *, n_in_blocks):
    keep = pl.program_id(0) < n_in_blocks
    o_ref[...] = jnp.where(keep, w_ref[...], 0.0).astype(BF16)


def _cast_pad_rows(w, n_pad):
    n, cols = w.shape
    tr = _pick(math.gcd(n, n_pad), (256, 128, 64, 32, 16))
    n_in_blocks = n // tr
    return pl.pallas_call(
        functools.partial(_cast_pad_rows_kernel, n_in_blocks=n_in_blocks),
        grid=(n_pad // tr,),
        in_specs=[pl.BlockSpec((tr, cols), lambda i: (jnp.minimum(i, n_in_blocks - 1), 0))],
        out_specs=pl.BlockSpec((tr, cols), lambda i: (i, 0)),
        out_shape=jax.ShapeDtypeStruct((n_pad, cols), BF16),
        compiler_params=_cparams("parallel"),
        name="cast_pad_rows",
    )(w)


def _pad_to(a, axis, size):
    pad = size - a.shape[axis]
    if pad == 0:
        return a
    widths = [(0, 0)] * a.ndim
    widths[axis] = (0, pad)
    return jnp.pad(a, widths)


def kernel(x, norm1_w, w_in, gdn_conv_w, gdn_A_log, gdn_dt_bias, gdn_norm_w, ssm_conv_w, ssm_conv_b,
           ssm_A_log, ssm_dt_bias, ssm_D, ssm_norm_w, w_out, norm2_w, ffn_w_gate, ffn_w_up,
           ffn_conv_w, ffn_conv_b, ffn_w_down, norm_f_w):
    B, L, D = x.shape
    depth = w_in.shape[0]
    gh = gdn_A_log.shape[1]
    sh = ssm_A_log.shape[1]
    gw = gh * GDN_HEAD_DIM
    sw = sh * SSM_HEAD_DIM
    gn = (ssm_conv_w.shape[-1] - sw) // 2
    n_groups = gn // SSM_STATE
    hpg = sh // n_groups
    assert 2 * gh + sh <= LANES and L % CHUNK == 0
    T = B * L
    n_chunks = T // CHUNK
    f_dim = ffn_w_gate.shape[-1]
    f_pad = -(-f_dim // 1024) * 1024 if f_dim >= 1024 else f_dim

    o_gate = 3 * gw
    o_b = o_gate + gw
    o_a = o_b + gh
    o_z = o_a + gh
    o_xbc = o_z + sw
    o_dt = o_xbc + sw + 2 * gn
    z_col0 = o_b
    x_col0 = o_b + sw
    dt_col0 = 2 * gh

    h2 = x.reshape(T, D)
    for i in range(depth):
        w_t = jnp.swapaxes(w_in, 1, 2)[i].astype(BF16)
        w_small_t = _pad_to(jnp.concatenate([w_t[o_b:o_z], w_t[o_dt:]], axis=0), 0, LANES)
        no_conv = jnp.zeros((SHORT_CONV, gw + sw), F32)
        conv_w = jnp.concatenate([gdn_conv_w[i], no_conv, ssm_conv_w[i]], axis=1)
        conv_b = jnp.concatenate([jnp.zeros((x_col0,), F32), ssm_conv_b[i]])[None, :]

        proj, small = _in_proj(h2, norm1_w[i][None, :], w_t, w_small_t, conv_w, conv_b, seq=L, q_width=gw,
                               a_width=o_b, gap=o_z - o_b, n_out=o_b + (o_dt - o_z), x_col0=x_col0)

        bias_row = _pad_to(jnp.concatenate([jnp.zeros((gh,), F32), gdn_dt_bias[i], ssm_dt_bias[i]])[None, :], 1, LANES)
        alog_row = _pad_to(jnp.concatenate([jnp.zeros((gh,), F32), gdn_A_log[i], ssm_A_log[i]])[None, :], 1, LANES)
        c1, c2, c2t = _prep(small, bias_row, alog_row, gh)
        c2t3 = c2t.reshape(LANES, n_chunks, CHUNK)
        arow_all = c2t[dt_col0:dt_col0 + sh].reshape(n_groups, hpg, n_chunks, CHUNK)
        arow_all = arow_all.transpose(0, 2, 1, 3).reshape(n_groups, n_chunks, hpg * CHUNK)

        y_gdn = _gdn(proj, c1, c2t3, gdn_norm_w[i][None, :],
                     batch=B, seq=L, n_heads=gh, gate_col0=o_gate // GDN_HEAD_DIM)
        y_ssm = _ssd(proj, c1, c2, arow_all, jnp.repeat(ssm_D[i], SSM_HEAD_DIM)[None, :],
                     ssm_norm_w[i][None, :], batch=B, seq=L, n_groups=n_groups, heads_per_group=hpg,
                     z_col0=z_col0, x_col0=x_col0, dt_col0=dt_col0)

        h2 = _out_proj(h2, y_gdn, y_ssm, w_out[i].astype(BF16))

        act = _ffn_up(h2, norm2_w[i][None, :],
                      _cast_pad_cols(ffn_w_gate[i], f_pad), _cast_pad_cols(ffn_w_up[i], f_pad),
                      _pad_to(ffn_conv_w[i], 1, f_pad), _pad_to(ffn_conv_b[i][None, :], 1, f_pad), seq=L)
        h2 = _ffn_down(act, _cast_pad_rows(ffn_w_down[i], f_pad), h2, norm_f_w[None, :],
                       final_norm=(i == depth - 1))
    return h2.reshape(B, L, D)
```
